```python
import math
import jax, jax.numpy as jnp
from jax import lax
import numpy as np

D_MODEL = 2048
BATCH = 2
SEQ = 16384
DEPTH = 4
DEC_BATCH = 4
DEC_SEQ = 2048
PAST_LEN = 128

GRID_W = 64
N_BRANCH = 4
BRANCH_W = 512
N_HEADS = 4
NA_HEAD_DIM = 128
WIN_R = 8
WIN_C = 16
DIFF_D = 64
GLA_DK = 64
GLA_DV = 128
GLA_RANK = 16
GLA_TAU = 16.0
GLA_CHUNK = 64
MLA_Q_LORA = 384
MLA_KV_LORA = 128
MLA_NOPE = 128
MLA_ROPE = 64
MLA_V = 128
ROPE_THETA = 10000.0
Q_BLOCK = 128
EPS = 1e-6

NA_COLS = 3 * N_HEADS * NA_HEAD_DIM
DIFF_COLS = 3 * N_HEADS * 2 * DIFF_D
GLA_COLS = 2 * N_HEADS * GLA_DK + N_HEADS * GLA_DV + 2 * GLA_RANK
MLA_COLS = MLA_Q_LORA + MLA_KV_LORA + MLA_ROPE
Z_COLS = N_BRANCH * BRANCH_W
IN_COLS = NA_COLS + DIFF_COLS + GLA_COLS + MLA_COLS + Z_COLS
SPLIT_POINTS = [NA_COLS, NA_COLS + DIFF_COLS, NA_COLS + DIFF_COLS + GLA_COLS, NA_COLS + DIFF_COLS + GLA_COLS + MLA_COLS]

kernel_name = "hybrid_bidir_gated_branch_encoder"


def rms_norm(x, w):
    xf = x.astype(jnp.float32)
    y = xf * lax.rsqrt(jnp.mean(xf * xf, axis=-1, keepdims=True) + EPS)
    return (y * w.astype(jnp.float32)).astype(x.dtype)


def rope(x):
    s, d = x.shape[1], x.shape[-1]
    inv = ROPE_THETA ** (-jnp.arange(0, d, 2, dtype=jnp.float32) / d)
    ang = jnp.arange(s, dtype=jnp.float32)[:, None] * inv[None, :]
    shape = (s,) + (1,) * (x.ndim - 3) + (d // 2,)
    cos = jnp.cos(ang).reshape(shape).astype(x.dtype)
    sin = jnp.sin(ang).reshape(shape).astype(x.dtype)
    x1, x2 = x[..., : d // 2], x[..., d // 2:]
    return jnp.concatenate([x1 * cos - x2 * sin, x1 * sin + x2 * cos], axis=-1)


def neighborhood_attention(q, k, v, rpb):
    b, s, h, dh = q.shape
    rows = s // GRID_W
    wr = min(WIN_R, rows)
    r = jnp.arange(rows)
    c = jnp.arange(GRID_W)
    rs = jnp.clip(r - wr // 2, 0, rows - wr)
    cs = jnp.clip(c - WIN_C // 2, 0, GRID_W - WIN_C)
    ri = rs[:, None] + jnp.arange(wr)[None, :]
    kg = k.reshape(b, rows, GRID_W, h, dh)[:, ri]
    vg = v.reshape(b, rows, GRID_W, h, dh)[:, ri]
    qg = q.reshape(b, rows, GRID_W, h, dh) * (dh ** -0.5)
    sc = jnp.einsum('brqhd,brwkhd->bhrqwk', qg, kg).astype(jnp.float32)
    dr_idx = ri - r[:, None] + (WIN_R - 1)
    dc_idx = jnp.clip(c[None, :] - c[:, None] + (WIN_C - 1), 0, 2 * WIN_C - 2)
    bias = rpb.astype(jnp.float32)[:, dr_idx[:, None, :, None], dc_idx[None, :, None, :]]
    in_win = (c[None, :] >= cs[:, None]) & (c[None, :] < cs[:, None] + WIN_C)
    sc = jnp.where(in_win[:, None, :], sc + bias[None], -jnp.inf)
    p = jax.nn.softmax(sc.reshape(b, h, rows, GRID_W, wr * GRID_W), axis=-1).reshape(sc.shape)
    o = jnp.einsum('bhrqwk,brwkhd->brqhd', p.astype(v.dtype), vg)
    return o.reshape(b, s, h * dh)


def diff_attention(q, k, v, lam):
    b, s, h, _, d = q.shape
    nb = s // Q_BLOCK
    qb = jnp.moveaxis((q * (d ** -0.5)).reshape(b, nb, Q_BLOCK, h, 2, d), 1, 0)

    def block(qi):
        sc = jnp.einsum('bqhmd,bkhmd->bhmqk', qi, k).astype(jnp.float32)
        p = jax.nn.softmax(sc, axis=-1)
        a = p[:, :, 0] - lam * p[:, :, 1]
        return jnp.einsum('bhqk,bkhv->bqhv', a.astype(v.dtype), v)

    o = lax.map(block, qb)
    return jnp.moveaxis(o, 0, 1).reshape(b, s, h, v.shape[-1])


def mla_attention(q_nope, q_rope, k_nope, k_rope, v):
    b, s, h, _ = q_nope.shape
    nb = s // Q_BLOCK
    scale = (MLA_NOPE + MLA_ROPE) ** -0.5
    qn = jnp.moveaxis((q_nope * scale).reshape(b, nb, Q_BLOCK, h, MLA_NOPE), 1, 0)
    qr = jnp.moveaxis((q_rope * scale).reshape(b, nb, Q_BLOCK, h, MLA_ROPE), 1, 0)

    def block(args):
        qn_i, qr_i = args
        sc = (jnp.einsum('bqhd,bkhd->bhqk', qn_i, k_nope)
              + jnp.einsum('bqhd,bkd->bhqk', qr_i, k_rope)).astype(jnp.float32)
        p = jax.nn.softmax(sc, axis=-1)
        return jnp.einsum('bhqk,bkhv->bqhv', p.astype(v.dtype), v)

    o = lax.map(block, (qn, qr))
    return jnp.moveaxis(o, 0, 1).reshape(b, s, h * MLA_V)


def gla_causal(q, k, v, g):
    b, t, h, dk = q.shape
    dv = v.shape[-1]
    c = GLA_CHUNK
    n = t // c
    q, k, v, g = [a.astype(jnp.float32).reshape(b, n, c, h, a.shape[-1]) for a in (q, k, v, g)]
    cum = jnp.cumsum(g, axis=2)
    ref = cum[:, :, c // 2 - 1: c // 2]
    a = jnp.einsum('bnihd,bnjhd->bnhij', q * jnp.exp(cum - ref), k * jnp.exp(ref - cum))
    a = jnp.where(jnp.tril(jnp.ones((c, c), dtype=bool)), a, 0.0)
    o_intra = jnp.einsum('bnhij,bnjhv->bnihv', a, v)
    last = cum[:, :, -1]
    ds = jnp.einsum('bnjhd,bnjhv->bnhdv', k * jnp.exp(last[:, :, None] - cum), v)

    def step(state, inp):
        ds_c, dec_c = inp
        return state * dec_c[..., None] + ds_c, state

    s0 = jnp.zeros((b, h, dk, dv), jnp.float32)
    _, s_prev = lax.scan(step, s0, (jnp.moveaxis(ds, 1, 0), jnp.moveaxis(jnp.exp(last), 1, 0)))
    o_inter = jnp.einsum('bnihd,nbhdv->bnihv', q * jnp.exp(cum), s_prev)
    return (o_intra + o_inter).reshape(b, t, h, dv)


def encoder_layer(x, layer_idx, pre_w, w_in, na_rpb, diff_lambda_qk, diff_subln_w,
                  gla_w_gate_up, gla_b_gate, gla_norm_w, mla_q_norm_w, mla_w_qb,
                  mla_kv_norm_w, mla_w_kvb, w_gate, w_branch, w_out, post_w):
    b, s, _ = x.shape
    h = rms_norm(x, pre_w)
    proj = h @ w_in
    na_in, diff_in, gla_in, mla_in, z = jnp.split(proj, SPLIT_POINTS, axis=-1)

    qa, ka, va = jnp.split(na_in, 3, axis=-1)
    hs = (b, s, N_HEADS, NA_HEAD_DIM)
    o_a = neighborhood_attention(qa.reshape(hs), ka.reshape(hs), va.reshape(hs), na_rpb)

    qd, kd, vd = jnp.split(diff_in, 3, axis=-1)
    qd = rope(qd.reshape(b, s, N_HEADS, 2, DIFF_D))
    kd = rope(kd.reshape(b, s, N_HEADS, 2, DIFF_D))
    lambda_init = 0.8 - 0.6 * math.exp(-0.3 * layer_idx)
    lq1, lk1, lq2, lk2 = diff_lambda_qk.astype(jnp.float32)
    lam = jnp.exp(jnp.sum(lq1 * lk1)) - jnp.exp(jnp.sum(lq2 * lk2)) + lambda_init
    o_b = diff_attention(qd, kd, vd.reshape(b, s, N_HEADS, 2 * DIFF_D), lam)
    o_b = (rms_norm(o_b, diff_subln_w) * (1.0 - lambda_init)).reshape(b, s, BRANCH_W)

    qc, kc, vc, gl = jnp.split(gla_in, [N_HEADS * GLA_DK, 2 * N_HEADS * GLA_DK,
                                         2 * N_HEADS * GLA_DK + N_HEADS * GLA_DV], axis=-1)
    gl = gl.reshape(b, s, 2, GLA_RANK)
    g = jax.nn.log_sigmoid((jnp.einsum('bsnr,nre->bsne', gl, gla_w_gate_up) + gla_b_gate).astype(jnp.float32)) / GLA_TAU
    g = g.reshape(b, s, 2, N_HEADS, GLA_DK)
    qc = qc.reshape(b, s, N_HEADS, GLA_DK) * (GLA_DK ** -0.5)
    kc = kc.reshape(b, s, N_HEADS, GLA_DK)
    vc = vc.reshape(b, s, N_HEADS, GLA_DV)
    o_fwd = gla_causal(qc, kc, vc, g[:, :, 0])
    o_bwd = gla_causal(qc[:, ::-1], kc[:, ::-1], vc[:, ::-1], g[:, ::-1, 1])[:, ::-1]
    o_c = rms_norm((o_fwd + o_bwd).astype(x.dtype), gla_norm_w).reshape(b, s, BRANCH_W)

    cq, ckv, kr = jnp.split(mla_in, [MLA_Q_LORA, MLA_Q_LORA + MLA_KV_LORA], axis=-1)
    qm = (rms_norm(cq, mla_q_norm_w) @ mla_w_qb).reshape(b, s, N_HEADS, MLA_NOPE + MLA_ROPE)
    q_nope, q_rope = qm[..., :MLA_NOPE], rope(qm[..., MLA_NOPE:])
    kv = (rms_norm(ckv, mla_kv_norm_w) @ mla_w_kvb).reshape(b, s, N_HEADS, MLA_NOPE + MLA_V)
    k_nope, vm = kv[..., :MLA_NOPE], kv[..., MLA_NOPE:]
    o_d = mla_attention(q_nope, q_rope, k_nope, rope(kr), vm)

    zg = jax.nn.silu(z).reshape(b, s, N_BRANCH, BRANCH_W)
    merged = None
    for i, o in enumerate((o_a, o_b, o_c, o_d)):
        term = jax.nn.sigmoid(h @ w_gate[i]) * ((o * zg[:, :, i]) @ w_branch[i])
        merged = term if merged is None else merged + term
    y = merged @ w_out
    return x + rms_norm(y, post_w)


def setup_inputs(seed: int = 0) -> dict:
    key = jax.random.key(seed)
    ks = jax.random.split(key, 18)

    def nrm(k, shape, scale):
        return jax.random.normal(k, shape, jnp.float32) * scale

    def gain(k, shape):
        return 1.0 + 0.01 * jax.random.normal(k, shape, jnp.float32)

    return {
        "x_prompt": nrm(ks[0], (BATCH, SEQ, D_MODEL), 1.0),
        "x_sample": nrm(ks[1], (DEC_BATCH, DEC_SEQ, D_MODEL), 1.0),
        "pre_norm_w": gain(ks[2], (DEPTH, D_MODEL)),
        "w_in": nrm(ks[3], (DEPTH, D_MODEL, IN_COLS), D_MODEL ** -0.5),
        "na_rpb": nrm(ks[4], (DEPTH, N_HEADS, 2 * WIN_R - 1, 2 * WIN_C - 1), 0.1),
        "diff_lambda_qk": nrm(ks[5], (DEPTH, 4, DIFF_D), 0.1),
        "diff_subln_w": gain(ks[6], (DEPTH, 2 * DIFF_D)),
        "gla_w_gate_up": nrm(ks[7], (DEPTH, 2, GLA_RANK, N_HEADS * GLA_DK), GLA_RANK ** -0.5),
        "gla_b_gate": nrm(ks[8], (DEPTH, 2, N_HEADS * GLA_DK), 0.1),
        "gla_norm_w": gain(ks[9], (DEPTH, GLA_DV)),
        "mla_q_norm_w": gain(ks[10], (DEPTH, MLA_Q_LORA)),
        "mla_w_qb": nrm(ks[11], (DEPTH, MLA_Q_LORA, N_HEADS * (MLA_NOPE + MLA_ROPE)), MLA_Q_LORA ** -0.5),
        "mla_kv_norm_w": gain(ks[12], (DEPTH, MLA_KV_LORA)),
        "mla_w_kvb": nrm(ks[13], (DEPTH, MLA_KV_LORA, N_HEADS * (MLA_NOPE + MLA_V)), MLA_KV_LORA ** -0.5),
        "w_gate": nrm(ks[14], (DEPTH, N_BRANCH, D_MODEL, D_MODEL), D_MODEL ** -0.5),
        "w_branch": nrm(ks[15], (DEPTH, N_BRANCH, BRANCH_W, D_MODEL), BRANCH_W ** -0.5),
        "w_out": nrm(ks[16], (DEPTH, D_MODEL, D_MODEL), D_MODEL ** -0.5),
        "post_norm_w": gain(ks[17], (DEPTH, D_MODEL)),
    }


def reference(x_prompt, x_sample, pre_norm_w, w_in, na_rpb, diff_lambda_qk, diff_subln_w,
              gla_w_gate_up, gla_b_gate, gla_norm_w, mla_q_norm_w, mla_w_qb, mla_kv_norm_w,
              mla_w_kvb, w_gate, w_branch, w_out, post_norm_w):
    params = (pre_norm_w, w_in, na_rpb, diff_lambda_qk, diff_subln_w, gla_w_gate_up,
              gla_b_gate, gla_norm_w, mla_q_norm_w, mla_w_qb, mla_kv_norm_w, mla_w_kvb,
              w_gate, w_branch, w_out, post_norm_w)

    def trunk(x):
        for l in range(DEPTH):
            x = encoder_layer(x, l, *[p[l] for p in params])
        return x

    y_prompt = trunk(x_prompt)
    y_sample = trunk(x_sample)
    return (y_prompt, y_sample)
```

```python
import functools
import math

import jax
import jax.numpy as jnp
from jax import lax
from jax.experimental import pallas as pl
from jax.experimental.pallas import tpu as pltpu

F32 = jnp.float32
BF16 = jnp.bfloat16

DEPTH = 4
GRID_W = 64
N_BRANCH = 4
BRANCH_W = 512
N_HEADS = 4
NA_HEAD_DIM = 128
WIN_R = 8
WIN_C = 16
DIFF_D = 64
GLA_DK = 64
GLA_DV = 128
GLA_RANK = 16
GLA_TAU = 16.0
GLA_CHUNK = 64
MLA_Q_LORA = 384
MLA_KV_LORA = 128
MLA_NOPE = 128
MLA_ROPE = 64
MLA_V = 128
ROPE_THETA = 10000.0
EPS = 1e-6
LOG2E = math.log2(math.e)

NA_COLS = 3 * N_HEADS * NA_HEAD_DIM
DIFF_COLS = 3 * N_HEADS * 2 * DIFF_D
GLA_QKV_COLS = 2 * N_HEADS * GLA_DK + N_HEADS * GLA_DV
GLA_COLS = GLA_QKV_COLS + 2 * GLA_RANK
MLA_COLS = MLA_Q_LORA + MLA_KV_LORA + MLA_ROPE
Z_COLS = N_BRANCH * BRANCH_W

LANES = 128
V7X_VMEM_BYTES = 64 * 1024 * 1024
VMEM_LIMIT = 56 * 1024 * 1024
MIB = 1024 * 1024

COL_NA = 0
COL_DIFF = COL_NA + NA_COLS
COL_GLA = COL_DIFF + DIFF_COLS
COL_MLA = COL_GLA + GLA_QKV_COLS
COL_Z = COL_MLA + MLA_Q_LORA + MLA_KV_LORA
COL_KRGL = COL_Z + Z_COLS
PROJ_TN = 1152
PROJ_COLS = 6 * PROJ_TN
GL_LANE0 = MLA_ROPE

NA_QROWS = 8
NA_KROWS = 16
NA_TQ = NA_QROWS * GRID_W
NA_TK = NA_KROWS * GRID_W
NEG_BIG = -1e30

VAUG_W = 2 * LANES
GLA_T = 256


def _cparams(sem, vmem=None):
    return pltpu.CompilerParams(dimension_semantics=sem, vmem_limit_bytes=vmem)


def _nt_dot(a, b):
    return lax.dot_general(a, b, (((1,), (1,)), ((), ())), preferred_element_type=F32)


def _tn_dot(a, b):
    return lax.dot_general(a, b, (((0,), (0,)), ((), ())), preferred_element_type=F32)


def _rmsnorm_kernel(x_ref, w_ref, o_ref):
    x = x_ref[...]
    y = x * lax.rsqrt(jnp.mean(x * x, axis=-1, keepdims=True) + EPS)
    o_ref[...] = (y * w_ref[...]).astype(o_ref.dtype)


def _rmsnorm(x, w):
    n, d = x.shape
    tm = min(512, n)
    return pl.pallas_call(
        _rmsnorm_kernel,
        out_shape=jax.ShapeDtypeStruct((n, d), BF16),
        grid=(n // tm,),
        in_specs=[pl.BlockSpec((tm, d), lambda i: (i, 0)),
                  pl.BlockSpec((1, d), lambda i: (0, 0))],
        out_specs=pl.BlockSpec((tm, d), lambda i: (i, 0)),
        compiler_params=_cparams(("parallel",)),
        name="pre_norm",
    )(x, w.reshape(1, d))


def _matmul_kernel(a_ref, b_ref, o_ref):
    o_ref[...] = jnp.dot(a_ref[...], b_ref[...], preferred_element_type=F32).astype(o_ref.dtype)


def _in_proj(h, w):
    n, d = h.shape
    c = w.shape[1]
    tm = min(1024, n)
    tn = PROJ_TN
    return pl.pallas_call(
        _matmul_kernel,
        out_shape=jax.ShapeDtypeStruct((n, c), BF16),
        grid=(n // tm, c // tn),
        in_specs=[pl.BlockSpec((tm, d), lambda i, j: (i, 0)),
                  pl.BlockSpec((d, tn), lambda i, j: (0, j))],
        out_specs=pl.BlockSpec((tm, tn), lambda i, j: (i, j)),
        compiler_params=_cparams(("parallel", "parallel"), VMEM_LIMIT),
        name="in_proj",
    )(h, w)


def _rope_lanes(x, c, s1, s2):
    return x * c + pltpu.roll(x, LANES - 32, 1) * s1 + pltpu.roll(x, 32, 1) * s2


def _prep_kernel(dq_ref, dk_ref, dv_ref, cqkv_ref, krgl_ref, cos_ref, s1_ref, s2_ref,
                 qnw_ref, wqb_ref, kvnw_ref, wkvb_ref,
                 dqo_ref, dko_ref, dvo_ref, mqo_ref, mko_ref, mvo_ref):
    c, s1, s2 = cos_ref[...], s1_ref[...], s2_ref[...]
    t = c.shape[0]
    lane = lax.broadcasted_iota(jnp.int32, (t, LANES), 1)
    ones_col = jnp.where(lane == 0, 1.0, 0.0).astype(BF16)
    diff_qscale = (DIFF_D ** -0.5) * LOG2E
    mla_qscale = ((MLA_NOPE + MLA_ROPE) ** -0.5) * LOG2E

    for hd in range(N_HEADS):
        sl = slice(hd * LANES, (hd + 1) * LANES)
        q = _rope_lanes(dq_ref[0, :, sl].astype(F32), c, s1, s2) * diff_qscale
        dqo_ref[0, :, sl] = q.astype(BF16)
        k = _rope_lanes(dk_ref[0, :, sl].astype(F32), c, s1, s2)
        dko_ref[0, :, sl] = k.astype(BF16)
        dvo_ref[0, :, hd * VAUG_W:hd * VAUG_W + LANES] = dv_ref[0, :, sl]
        dvo_ref[0, :, hd * VAUG_W + LANES:(hd + 1) * VAUG_W] = ones_col

    cq = cqkv_ref[0, :, :MLA_Q_LORA].astype(F32)
    cqn = cq * lax.rsqrt(jnp.mean(cq * cq, axis=-1, keepdims=True) + EPS) * qnw_ref[...]
    qm = jnp.dot(cqn.astype(BF16), wqb_ref[...], preferred_element_type=F32)
    ckv = cqkv_ref[0, :, MLA_Q_LORA:].astype(F32)
    ckvn = ckv * lax.rsqrt(jnp.mean(ckv * ckv, axis=-1, keepdims=True) + EPS) * kvnw_ref[...]
    kv = jnp.dot(ckvn.astype(BF16), wkvb_ref[...], preferred_element_type=F32)
    kr = jnp.where(lane < MLA_ROPE, krgl_ref[0].astype(F32), 0.0)
    kr = _rope_lanes(kr, c, s1, s2).astype(BF16)
    for hd in range(N_HEADS):
        lo, mid, hi = hd * VAUG_W, hd * VAUG_W + LANES, (hd + 1) * VAUG_W
        mqo_ref[0, :, lo:mid] = (qm[:, lo:mid] * mla_qscale).astype(BF16)
        qr = _rope_lanes(qm[:, mid:hi], c, s1, s2) * mla_qscale
        mqo_ref[0, :, mid:hi] = qr.astype(BF16)
        mko_ref[0, :, lo:mid] = kv[:, lo:mid].astype(BF16)
        mko_ref[0, :, mid:hi] = kr
        mvo_ref[0, :, lo:mid] = kv[:, mid:hi].astype(BF16)
        mvo_ref[0, :, mid:hi] = ones_col


def _prep(proj3, tables, qnw, wqb, kvnw, wkvb):
    b, s, _ = proj3.shape
    tm = min(512, s)
    cos, s1, s2 = tables
    hw = N_HEADS * LANES
    aw = N_HEADS * VAUG_W
    tok = lambda cb: (lambda bi, i: (bi, i, cb))
    tab = pl.BlockSpec((tm, LANES), lambda bi, i: (i, 0))
    const = lambda shape: pl.BlockSpec(shape, lambda bi, i: (0, 0))
    out_sd = lambda w: jax.ShapeDtypeStruct((b, s, w), BF16)
    out_bs = lambda w: pl.BlockSpec((1, tm, w), lambda bi, i: (bi, i, 0))
    return pl.pallas_call(
        _prep_kernel,
        out_shape=(out_sd(hw), out_sd(hw), out_sd(aw), out_sd(aw), out_sd(aw), out_sd(aw)),
        grid=(b, s // tm),
        in_specs=[pl.BlockSpec((1, tm, hw), tok(COL_DIFF // hw)),
                  pl.BlockSpec((1, tm, hw), tok(COL_DIFF // hw + 1)),
                  pl.BlockSpec((1, tm, hw), tok(COL_DIFF // hw + 2)),
                  pl.BlockSpec((1, tm, hw), tok(COL_MLA // hw)),
                  pl.BlockSpec((1, tm, LANES), tok(COL_KRGL // LANES)),
                  tab, tab, tab,
                  const((1, MLA_Q_LORA)), const(wqb.shape),
                  const((1, MLA_KV_LORA)), const(wkvb.shape)],
        out_specs=(out_bs(hw), out_bs(hw), out_bs(aw), out_bs(aw), out_bs(aw), out_bs(aw)),
        compiler_params=_cparams(("parallel", "parallel"), VMEM_LIMIT),
        name="branch_prep",
    )(proj3, proj3, proj3, proj3, proj3, cos, s1, s2,
      qnw.reshape(1, -1), wqb, kvnw.reshape(1, -1), wkvb)


def _flash_kernel(*refs, n_maps, tk, lam_init):
    if n_maps == 2:
        q_ref, k_ref, v_ref, lam_ref, subw_ref, o_ref, m_scr, acc_scr = refs
    else:
        q_ref, k_ref, v_ref, o_ref, m_scr, acc_scr = refs
    tq = q_ref.shape[1]
    nk = k_ref.shape[1] // tk
    q = q_ref[0]
    if n_maps == 2:
        lane = lax.broadcasted_iota(jnp.int32, q.shape, 1)
        zero = jnp.zeros_like(q)
        qs = (jnp.where(lane < DIFF_D, q, zero), jnp.where(lane >= DIFF_D, q, zero))
    else:
        qs = (q,)

    m_scr[...] = jnp.full(m_scr.shape, -jnp.inf, F32)
    acc_scr[...] = jnp.zeros(acc_scr.shape, F32)

    def body(j, carry):
        start = pl.multiple_of(j * tk, tk)
        k = k_ref[0, pl.ds(start, tk), :]
        v = v_ref[0, pl.ds(start, tk), :]
        for mi in range(n_maps):
            s = _nt_dot(qs[mi], k)
            m_prev = m_scr[mi]
            m_new = jnp.maximum(m_prev, jnp.max(s, axis=1, keepdims=True))
            alpha = jnp.exp2(m_prev - m_new)
            p = jnp.exp2(s - pltpu.repeat(m_new, tk // LANES, 1))
            pv = jnp.dot(p.astype(BF16), v, preferred_element_type=F32)
            acc_scr[mi] = acc_scr[mi] * pltpu.repeat(alpha, VAUG_W // LANES, 1) + pv
            m_scr[mi] = m_new
        return carry

    lax.fori_loop(0, nk, body, 0)

    outs = []
    for mi in range(n_maps):
        acc = acc_scr[mi]
        outs.append(acc[:, :LANES] / acc[:, LANES:LANES + 1])
    if n_maps == 2:
        lq = lam_ref[...]
        lam = (jnp.exp(jnp.sum(lq[0:1] * lq[1:2], axis=1, keepdims=True))
               - jnp.exp(jnp.sum(lq[2:3] * lq[3:4], axis=1, keepdims=True)) + lam_init)
        o = outs[0] - lam * outs[1]
        y = o * lax.rsqrt(jnp.mean(o * o, axis=-1, keepdims=True) + EPS) * subw_ref[...]
        o = y * (1.0 - lam_init)
    else:
        o = outs[0]
    o_ref[0] = o.astype(o_ref.dtype)


def _flash(q, k, vaug, *, n_maps, lam_qk=None, subw=None, lam_init=0.0):
    b, s, qw_all = q.shape
    qw = qw_all // N_HEADS
    kw = k.shape[2] // N_HEADS
    tq = min(512, s)
    tk = min(1024, s)
    in_specs = [pl.BlockSpec((1, tq, qw), lambda bi, hd, i: (bi, i, hd)),
                pl.BlockSpec((1, s, kw), lambda bi, hd, i: (bi, 0, hd)),
                pl.BlockSpec((1, s, VAUG_W), lambda bi, hd, i: (bi, 0, hd))]
    args = [q, k, vaug]
    if n_maps == 2:
        in_specs += [pl.BlockSpec(lam_qk.shape, lambda bi, hd, i: (0, 0)),
                     pl.BlockSpec((1, LANES), lambda bi, hd, i: (0, 0))]
        args += [lam_qk, subw.reshape(1, LANES)]
    return pl.pallas_call(
        functools.partial(_flash_kernel, n_maps=n_maps, tk=tk, lam_init=lam_init),
        out_shape=jax.ShapeDtypeStruct((b, s, N_HEADS * LANES), BF16),
        grid=(b, N_HEADS, s // tq),
        in_specs=in_specs,
        out_specs=pl.BlockSpec((1, tq, LANES), lambda bi, hd, i: (bi, i, hd)),
        scratch_shapes=[pltpu.VMEM((n_maps, tq, LANES), F32),
                        pltpu.VMEM((n_maps, tq, VAUG_W), F32)],
        compiler_params=_cparams(("parallel", "parallel", "parallel"), VMEM_LIMIT),
        name="diff_attention" if n_maps == 2 else "latent_attention",
    )(*args)


def _na_kernel(q_ref, k_ref, v_ref, b_ref, o_ref, *, rows):
    qi = pl.program_id(2)
    ks_row = jnp.clip(qi * NA_QROWS - WIN_R // 2, 0, rows - NA_KROWS)
    ks = pl.multiple_of(ks_row * GRID_W, 256)
    k = k_ref[0, pl.ds(ks, NA_TK), :]
    v = v_ref[0, pl.ds(ks, NA_TK), :]
    s = _nt_dot(q_ref[0], k) * (NA_HEAD_DIM ** -0.5) + b_ref[0, 0]
    m = jnp.max(s, axis=1, keepdims=True)
    p = jnp.exp(s - m)
    l = jnp.sum(p, axis=1, keepdims=True)
    o = jnp.dot(p.astype(BF16), v, preferred_element_type=F32) / l
    o_ref[0] = o.astype(o_ref.dtype)


def _na_bias_tiles(rpb):
    a = jnp.arange(NA_QROWS)
    w = jnp.arange(NA_KROWS)
    c = jnp.arange(GRID_W)
    cs = jnp.clip(c - WIN_C // 2, 0, GRID_W - WIN_C)
    col_ok = (c[None, :] >= cs[:, None]) & (c[None, :] < cs[:, None] + WIN_C)
    dc = jnp.clip(c[None, :] - c[:, None] + (WIN_C - 1), 0, 2 * WIN_C - 2)
    half = WIN_R // 2
    tiles = []
    for off, wstart in ((0, jnp.maximum(a - half, 0)),
                        (half, a),
                        (NA_QROWS, jnp.minimum(a + half, NA_QROWS))):
        row_ok = (w[None, :] >= wstart[:, None]) & (w[None, :] < wstart[:, None] + WIN_R)
        dr = jnp.clip(w[None, :] - a[:, None] - off + (WIN_R - 1), 0, 2 * WIN_R - 2)
        bias = rpb.astype(F32)[:, dr[:, None, :, None], dc[None, :, None, :]]
        ok = row_ok[:, None, :, None] & col_ok[None, :, None, :]
        tiles.append(jnp.where(ok[None], bias, NEG_BIG).reshape(N_HEADS, NA_TQ, NA_TK))
    return jnp.stack(tiles)


def _na(proj3, bias_tiles):
    b, s, _ = proj3.shape
    rows = s // GRID_W
    assert rows % NA_QROWS == 0 and rows >= NA_KROWS
    nq = rows // NA_QROWS
    qb, kb, vb = COL_NA // LANES, COL_NA // LANES + N_HEADS, COL_NA // LANES + 2 * N_HEADS

    def bias_idx(bi, hd, i):
        kind = jnp.where(i == 0, 0, jnp.where(i == nq - 1, 2, 1))
        return (kind, hd, 0, 0)

    return pl.pallas_call(
        functools.partial(_na_kernel, rows=rows),
        out_shape=jax.ShapeDtypeStruct((b, s, N_HEADS * LANES), BF16),
        grid=(b, N_HEADS, nq),
        in_specs=[pl.BlockSpec((1, NA_TQ, LANES), lambda bi, hd, i: (bi, i, qb + hd)),
                  pl.BlockSpec((1, s, LANES), lambda bi, hd, i: (bi, 0, kb + hd)),
                  pl.BlockSpec((1, s, LANES), lambda bi, hd, i: (bi, 0, vb + hd)),
                  pl.BlockSpec((1, 1, NA_TQ, NA_TK), bias_idx)],
        out_specs=pl.BlockSpec((1, NA_TQ, LANES), lambda bi, hd, i: (bi, i, hd)),
        compiler_params=_cparams(("parallel", "parallel", "parallel"), VMEM_LIMIT),
        name="neighborhood_attention",
    )(proj3, proj3, proj3, bias_tiles)


def _gla_kernel(*refs, reverse):
    if reverse:
        q_ref, k_ref, v_ref, gl_ref, wg_ref, bg_ref, ofwd_ref, nw_ref, o_ref, st_scr = refs
    else:
        q_ref, k_ref, v_ref, gl_ref, wg_ref, bg_ref, o_ref, st_scr = refs
    t = q_ref.shape[1]
    gw = N_HEADS * GLA_DK
    c_len = GLA_CHUNK

    @pl.when(pl.program_id(1) == 0)
    def _():
        st_scr[...] = jnp.zeros(st_scr.shape, F32)

    gpre = jnp.dot(gl_ref[0], wg_ref[...], preferred_element_type=F32) + bg_ref[...]
    g = (jnp.minimum(gpre, 0.0) - jnp.log1p(jnp.exp(-jnp.abs(gpre)))) * (1.0 / GLA_TAU)

    pos = lax.broadcasted_iota(jnp.int32, (t, gw), 0) & (c_len - 1)
    cum = g
    for sh in (1, 2, 4, 8, 16, 32):
        if reverse:
            cum = cum + jnp.where(pos < c_len - sh, pltpu.roll(cum, t - sh, 0), 0.0)
        else:
            cum = cum + jnp.where(pos >= sh, pltpu.roll(cum, sh, 0), 0.0)

    qf = q_ref[0].astype(F32) * (GLA_DK ** -0.5)
    kf = k_ref[0].astype(F32)
    ref_i = c_len // 2 if reverse else c_len // 2 - 1
    last_i = 0 if reverse else c_len - 1
    ri = lax.broadcasted_iota(jnp.int32, (c_len, c_len), 0)
    ci = lax.broadcasted_iota(jnp.int32, (c_len, c_len), 1)
    tri = (ci >= ri) if reverse else (ri >= ci)
    lane = lax.broadcasted_iota(jnp.int32, (c_len, LANES), 1)
    n_chunks = t // c_len
    order = range(n_chunks - 1, -1, -1) if reverse else range(n_chunks)
    for n in order:
        r0 = n * c_len
        cn = cum[r0:r0 + c_len]
        ref = cum[r0 + ref_i:r0 + ref_i + 1]
        last = cum[r0 + last_i:r0 + last_i + 1]
        qn, kn = qf[r0:r0 + c_len], kf[r0:r0 + c_len]
        qd = qn * jnp.exp(cn - ref)
        kd = (kn * jnp.exp(ref - cn)).astype(BF16)
        qe = qn * jnp.exp(cn)
        kl = (kn * jnp.exp(last - cn)).astype(BF16)
        dec = jnp.exp(last)
        for hd in range(N_HEADS):
            lsl = slice((hd // 2) * LANES, (hd // 2 + 1) * LANES)
            own = (lane < GLA_DK) if hd % 2 == 0 else (lane >= GLA_DK)
            vsl = slice(hd * GLA_DV, (hd + 1) * GLA_DV)
            v_h = v_ref[0, r0:r0 + c_len, vsl]
            a = _nt_dot(jnp.where(own, qd[:, lsl], 0.0).astype(BF16), kd[:, lsl])
            a = jnp.where(tri, a, 0.0)
            o_h = jnp.dot(a.astype(BF16), v_h, preferred_element_type=F32)
            st = st_scr[hd]
            o_h = o_h + _nt_dot(jnp.where(own, qe[:, lsl], 0.0).astype(BF16), st.astype(BF16))
            st_scr[hd] = st * dec[:, lsl] + _tn_dot(v_h, kl[:, lsl])
            if reverse:
                tot = o_h + ofwd_ref[0, r0:r0 + c_len, vsl]
                y = tot * lax.rsqrt(jnp.mean(tot * tot, axis=-1, keepdims=True) + EPS) * nw_ref[...]
                o_ref[0, r0:r0 + c_len, vsl] = y.astype(o_ref.dtype)
            else:
                o_ref[0, r0:r0 + c_len, vsl] = o_h


def _gla_pass(proj3, wg, bg, reverse, o_fwd=None, norm_w=None):
    b, s, _ = proj3.shape
    t = min(GLA_T, s)
    nb = s // t
    gw = N_HEADS * GLA_DK
    vw = N_HEADS * GLA_DV
    blk = (lambda i: nb - 1 - i) if reverse else (lambda i: i)
    tok = lambda cb: (lambda bi, i: (bi, blk(i), cb))
    const = lambda shape: pl.BlockSpec(shape, lambda bi, i: (0, 0))
    in_specs = [pl.BlockSpec((1, t, gw), tok(COL_GLA // gw)),
                pl.BlockSpec((1, t, gw), tok(COL_GLA // gw + 1)),
                pl.BlockSpec((1, t, vw), tok((COL_GLA + 2 * gw) // vw)),
                pl.BlockSpec((1, t, LANES), tok(COL_KRGL // LANES)),
                const((LANES, gw)), const((1, gw))]
    args = [proj3, proj3, proj3, proj3, wg, bg]
    if reverse:
        in_specs += [pl.BlockSpec((1, t, vw), tok(0)), const((1, GLA_DV))]
        args += [o_fwd, norm_w.reshape(1, GLA_DV)]
    return pl.pallas_call(
        functools.partial(_gla_kernel, reverse=reverse),
        out_shape=jax.ShapeDtypeStruct((b, s, vw), BF16 if reverse else F32),
        grid=(b, nb),
        in_specs=in_specs,
        out_specs=pl.BlockSpec((1, t, vw), tok(0)),
        scratch_shapes=[pltpu.VMEM((N_HEADS, GLA_DV, LANES), F32)],
        compiler_params=_cparams(("parallel", "arbitrary")),
        name="gla_backward" if reverse else "gla_forward",
    )(*args)


def _gla(proj3, w_gate_up, b_gate, norm_w):
    gw = N_HEADS * GLA_DK
    wgs = []
    for d in range(2):
        lo = GL_LANE0 + d * GLA_RANK
        wgs.append(jnp.zeros((LANES, gw), F32).at[lo:lo + GLA_RANK].set(w_gate_up[d]).astype(BF16))
    o_fwd = _gla_pass(proj3, wgs[0], b_gate[0].reshape(1, gw), False)
    return _gla_pass(proj3, wgs[1], b_gate[1].reshape(1, gw), True, o_fwd, norm_w)


def _merge_kernel(h_ref, oa_ref, ob_ref, oc_ref, od_ref, za_ref, zb_ref, zc_ref, zd_ref,
                  wg_ref, wb_ref, o_ref):
    h = h_ref[...]
    acc = None
    for i, (o_r, z_r) in enumerate(((oa_ref, za_ref), (ob_ref, zb_ref), (oc_ref, zc_ref), (od_ref, zd_ref))):
        z = z_r[...].astype(F32)
        u = (o_r[...].astype(F32) * (z * jax.nn.sigmoid(z))).astype(BF16)
        t = jnp.dot(u, wb_ref[i], preferred_element_type=F32)
        gate = jax.nn.sigmoid(jnp.dot(h, wg_ref[i], preferred_element_type=F32))
        acc = gate * t if acc is None else acc + gate * t
    o_ref[...] = acc.astype(o_ref.dtype)


def _merge(h, branches, proj, wg, wb):
    n, d = h.shape
    tm = min(512, n)
    tn = min(512, d)
    zb0 = COL_Z // BRANCH_W
    tok = pl.BlockSpec((tm, BRANCH_W), lambda j, i: (i, 0))
    zspec = lambda k: pl.BlockSpec((tm, BRANCH_W), lambda j, i: (i, zb0 + k))
    return pl.pallas_call(
        _merge_kernel,
        out_shape=jax.ShapeDtypeStruct((n, d), BF16),
        grid=(d // tn, n // tm),
        in_specs=[pl.BlockSpec((tm, d), lambda j, i: (i, 0)), tok, tok, tok, tok,
                  zspec(0), zspec(1), zspec(2), zspec(3),
                  pl.BlockSpec((N_BRANCH, d, tn), lambda j, i: (0, 0, j)),
                  pl.BlockSpec((N_BRANCH, BRANCH_W, tn), lambda j, i: (0, 0, j))],
        out_specs=pl.BlockSpec((tm, tn), lambda j, i: (i, j)),
        compiler_params=_cparams(("parallel", "parallel"), VMEM_LIMIT),
        name="branch_merge",
    )(h, *branches, proj, proj, proj, proj, wg, wb)


def _out_kernel(m_ref, w_ref, x_ref, pw_ref, nw_ref, xo_ref, ho_ref):
    y = jnp.dot(m_ref[...], w_ref[...], preferred_element_type=F32)
    y = y * lax.rsqrt(jnp.mean(y * y, axis=-1, keepdims=True) + EPS) * pw_ref[...]
    xn = x_ref[...] + y
    xo_ref[...] = xn
    hn = xn * lax.rsqrt(jnp.mean(xn * xn, axis=-1, keepdims=True) + EPS) * nw_ref[...]
    ho_ref[...] = hn.astype(ho_ref.dtype)


def _out_proj(merged, w_out, x, post_w, next_pre_w):
    n, d = x.shape
    tm = min(256, n)
    tok = lambda: pl.BlockSpec((tm, d), lambda i: (i, 0))
    vec = lambda: pl.BlockSpec((1, d), lambda i: (0, 0))
    return pl.pallas_call(
        _out_kernel,
        out_shape=(jax.ShapeDtypeStruct((n, d), F32), jax.ShapeDtypeStruct((n, d), BF16)),
        grid=(n // tm,),
        in_specs=[tok(), pl.BlockSpec((d, d), lambda i: (0, 0)), tok(), vec(), vec()],
        out_specs=(tok(), tok()),
        compiler_params=_cparams(("parallel",), VMEM_LIMIT),
        name="out_proj",
    )(merged, w_out, x, post_w.reshape(1, d), next_pre_w.reshape(1, d))


def _permute_w_in(w_in):
    depth, d, _ = w_in.shape
    na = w_in[:, :, :NA_COLS]
    diff = w_in[:, :, NA_COLS:NA_COLS + DIFF_COLS]
    g0 = NA_COLS + DIFF_COLS
    gla_qkv = w_in[:, :, g0:g0 + GLA_QKV_COLS]
    gl = w_in[:, :, g0 + GLA_QKV_COLS:g0 + GLA_COLS]
    m0 = g0 + GLA_COLS
    cq_ckv = w_in[:, :, m0:m0 + MLA_Q_LORA + MLA_KV_LORA]
    kr = w_in[:, :, m0 + MLA_Q_LORA + MLA_KV_LORA:m0 + MLA_COLS]
    z = w_in[:, :, m0 + MLA_COLS:]
    used = COL_KRGL + MLA_ROPE + 2 * GLA_RANK
    pad = jnp.zeros((depth, d, PROJ_COLS - used), w_in.dtype)
    return jnp.concatenate([na, diff, gla_qkv, cq_ckv, z, kr, gl, pad], axis=-1).astype(BF16)


def _pad_w_qb(w_qb):
    depth, r, _ = w_qb.shape
    w = w_qb.reshape(depth, r, N_HEADS, MLA_NOPE + MLA_ROPE)
    w = jnp.pad(w, ((0, 0), (0, 0), (0, 0), (0, VAUG_W - MLA_NOPE - MLA_ROPE)))
    return w.reshape(depth, r, N_HEADS * VAUG_W).astype(BF16)


def _rope_tables(s):
    half = DIFF_D // 2
    inv = ROPE_THETA ** (-jnp.arange(0, DIFF_D, 2, dtype=F32) / DIFF_D)
    ang = jnp.arange(s, dtype=F32)[:, None] * inv[None, :]
    cos, sin = jnp.cos(ang), jnp.sin(ang)
    zero = jnp.zeros_like(sin)
    reps = LANES // DIFF_D
    c = jnp.tile(jnp.concatenate([cos, cos], axis=1), (1, reps))
    s1 = jnp.tile(jnp.concatenate([-sin, zero], axis=1), (1, reps))
    s2 = jnp.tile(jnp.concatenate([zero, sin], axis=1), (1, reps))
    assert half * 2 * reps == LANES
    return c, s1, s2


def _trunk(x, p):
    b, s, d = x.shape
    n = b * s
    tables = _rope_tables(s)
    xf = x.reshape(n, d)
    h = _rmsnorm(xf, p["pre_norm_w"][0])
    for l in range(DEPTH):
        lam_init = 0.8 - 0.6 * math.exp(-0.3 * l)
        proj = _in_proj(h, p["w_in"][l])
        proj3 = proj.reshape(b, s, PROJ_COLS)
        dq, dk, dva, mq, mk, mva = _prep(proj3, tables, p["mla_q_norm_w"][l], p["mla_w_qb"][l],
                                         p["mla_kv_norm_w"][l], p["mla_w_kvb"][l])
        o_a = _na(proj3, p["na_bias"][l])
        o_b = _flash(dq, dk, dva, n_maps=2, lam_qk=p["diff_lambda_qk"][l],
                     subw=p["diff_subln_w"][l], lam_init=lam_init)
        o_c = _gla(proj3, p["gla_w_gate_up"][l], p["gla_b_gate"][l], p["gla_norm_w"][l])
        o_d = _flash(mq, mk, mva, n_maps=1)
        branches = [o.reshape(n, BRANCH_W) for o in (o_a, o_b, o_c, o_d)]
        merged = _merge(h, branches, proj, p["w_gate"][l], p["w_branch"][l])
        next_pre = p["pre_norm_w"][(l + 1) % DEPTH]
        xf, h = _out_proj(merged, p["w_out"][l], xf, p["post_norm_w"][l], next_pre)
    return xf.reshape(b, s, d)


def kernel(x_prompt, x_sample, pre_norm_w, w_in, na_rpb, diff_lambda_qk, diff_subln_w, gla_w_gate_up, gla_b_gate, gla_norm_w, mla_q_norm_w, mla_w_qb, mla_kv_norm_w, mla_w_kvb, w_gate, w_branch, w_out, post_norm_w):
    p = {
        "pre_norm_w": pre_norm_w,
        "w_in": _permute_w_in(w_in),
        "na_bias": jnp.stack([_na_bias_tiles(na_rpb[l]) for l in range(DEPTH)]),
        "diff_lambda_qk": diff_lambda_qk,
        "diff_subln_w": diff_subln_w,
        "gla_w_gate_up": gla_w_gate_up,
        "gla_b_gate": gla_b_gate,
        "gla_norm_w": gla_norm_w,
        "mla_q_norm_w": mla_q_norm_w,
        "mla_w_qb": _pad_w_qb(mla_w_qb),
        "mla_kv_norm_w": mla_kv_norm_w,
        "mla_w_kvb": mla_w_kvb.astype(BF16),
        "w_gate": w_gate.astype(BF16),
        "w_branch": w_branch.astype(BF16),
        "w_out": w_out.astype(BF16),
        "post_norm_w": post_norm_w,
    }
    return (_trunk(x_prompt, p), _trunk(x_sample, p))
```

```python
import functools
import math

import jax
import jax.numpy as jnp
import numpy as np
from jax import lax
from jax.experimental import pallas as pl
from jax.experimental.pallas import tpu as pltpu

F32 = jnp.float32
BF16 = jnp.bfloat16

DEPTH = 4
GRID_W = 64
N_BRANCH = 4
BRANCH_W = 512
N_HEADS = 4
NA_HEAD_DIM = 128
WIN_R = 8
WIN_C = 16
DIFF_D = 64
GLA_DK = 64
GLA_DV = 128
GLA_RANK = 16
GLA_TAU = 16.0
GLA_CHUNK = 64
MLA_Q_LORA = 384
MLA_KV_LORA = 128
MLA_NOPE = 128
MLA_ROPE = 64
MLA_V = 128
ROPE_THETA = 10000.0
EPS = 1e-6
LOG2E = math.log2(math.e)

NA_COLS = 3 * N_HEADS * NA_HEAD_DIM
DIFF_COLS = 3 * N_HEADS * 2 * DIFF_D
GLA_QKV_COLS = 2 * N_HEADS * GLA_DK + N_HEADS * GLA_DV
GLA_COLS = GLA_QKV_COLS + 2 * GLA_RANK
MLA_COLS = MLA_Q_LORA + MLA_KV_LORA + MLA_ROPE
Z_COLS = N_BRANCH * BRANCH_W

LANES = 128
V7X_VMEM_BYTES = 64 * 1024 * 1024
VMEM_LIMIT = 56 * 1024 * 1024
MIB = 1024 * 1024

COL_NA = 0
COL_DIFF = COL_NA + NA_COLS
COL_GLA = COL_DIFF + DIFF_COLS
COL_MLA = COL_GLA + GLA_QKV_COLS
COL_Z = COL_MLA + MLA_Q_LORA + MLA_KV_LORA
COL_KRGL = COL_Z + Z_COLS
PROJ_TN = 1152
PROJ_COLS = 6 * PROJ_TN
GL_LANE0 = MLA_ROPE

NA_QROWS = 8
NA_KROWS = 16
NA_TQ = NA_QROWS * GRID_W
NA_TK = NA_KROWS * GRID_W
NEG_BIG = -1e30

VAUG_W = 2 * LANES
GLA_T = 256


def _cparams(sem, vmem=None):
    return pltpu.CompilerParams(dimension_semantics=sem, vmem_limit_bytes=vmem)


def _nt_dot(a, b):
    return lax.dot_general(a, b, (((1,), (1,)), ((), ())), preferred_element_type=F32)


def _tn_dot(a, b):
    return lax.dot_general(a, b, (((0,), (0,)), ((), ())), preferred_element_type=F32)


def _rmsnorm_kernel(x_ref, w_ref, o_ref):
    x = x_ref[...]
    y = x * lax.rsqrt(jnp.mean(x * x, axis=-1, keepdims=True) + EPS)
    o_ref[...] = (y * w_ref[...]).astype(o_ref.dtype)


def _rmsnorm(x, w):
    n, d = x.shape
    tm = min(512, n)
    return pl.pallas_call(
        _rmsnorm_kernel,
        out_shape=jax.ShapeDtypeStruct((n, d), BF16),
        grid=(n // tm,),
        in_specs=[pl.BlockSpec((tm, d), lambda i: (i, 0)),
                  pl.BlockSpec((1, d), lambda i: (0, 0))],
        out_specs=pl.BlockSpec((tm, d), lambda i: (i, 0)),
        compiler_params=_cparams(("parallel",)),
        name="pre_norm",
    )(x, w.reshape(1, d))


def _matmul_kernel(a_ref, b_ref, o_ref):
    o_ref[...] = jnp.dot(a_ref[...], b_ref[...], preferred_element_type=F32).astype(o_ref.dtype)


def _in_proj(h, w):
    n, d = h.shape
    c = w.shape[1]
    tm = min(1024, n)
    tn = PROJ_TN
    return pl.pallas_call(
        _matmul_kernel,
        out_shape=jax.ShapeDtypeStruct((n, c), BF16),
        grid=(n // tm, c // tn),
        in_specs=[pl.BlockSpec((tm, d), lambda i, j: (i, 0)),
                  pl.BlockSpec((d, tn), lambda i, j: (0, j))],
        out_specs=pl.BlockSpec((tm, tn), lambda i, j: (i, j)),
        compiler_params=_cparams(("parallel", "parallel"), VMEM_LIMIT),
        name="in_proj",
    )(h, w)


def _rope_lanes(x, c, s1, s2):
    return x * c + pltpu.roll(x, LANES - 32, 1) * s1 + pltpu.roll(x, 32, 1) * s2


def _prep_kernel(dq_ref, dk_ref, dv_ref, cqkv_ref, krgl_ref, cos_ref, s1_ref, s2_ref,
                 qnw_ref, wqb_ref, kvnw_ref, wkvb_ref,
                 dqo_ref, dko_ref, dvo_ref, mqo_ref, mko_ref, mvo_ref):
    c, s1, s2 = cos_ref[...], s1_ref[...], s2_ref[...]
    t = c.shape[0]
    lane = lax.broadcasted_iota(jnp.int32, (t, LANES), 1)
    ones_col = jnp.where(lane == 0, 1.0, 0.0).astype(BF16)
    diff_qscale = (DIFF_D ** -0.5) * LOG2E
    mla_qscale = ((MLA_NOPE + MLA_ROPE) ** -0.5) * LOG2E

    for hd in range(N_HEADS):
        sl = slice(hd * LANES, (hd + 1) * LANES)
        q = _rope_lanes(dq_ref[0, :, sl].astype(F32), c, s1, s2) * diff_qscale
        dqo_ref[0, :, sl] = q.astype(BF16)
        k = _rope_lanes(dk_ref[0, :, sl].astype(F32), c, s1, s2)
        dko_ref[0, :, sl] = k.astype(BF16)
        dvo_ref[0, :, hd * VAUG_W:hd * VAUG_W + LANES] = dv_ref[0, :, sl]
        dvo_ref[0, :, hd * VAUG_W + LANES:(hd + 1) * VAUG_W] = ones_col

    cq = cqkv_ref[0, :, :MLA_Q_LORA].astype(F32)
    cqn = cq * lax.rsqrt(jnp.mean(cq * cq, axis=-1, keepdims=True) + EPS) * qnw_ref[...]
    qm = jnp.dot(cqn.astype(BF16), wqb_ref[...], preferred_element_type=F32)
    ckv = cqkv_ref[0, :, MLA_Q_LORA:].astype(F32)
    ckvn = ckv * lax.rsqrt(jnp.mean(ckv * ckv, axis=-1, keepdims=True) + EPS) * kvnw_ref[...]
    kv = jnp.dot(ckvn.astype(BF16), wkvb_ref[...], preferred_element_type=F32)
    kr = jnp.where(lane < MLA_ROPE, krgl_ref[0].astype(F32), 0.0)
    kr = _rope_lanes(kr, c, s1, s2).astype(BF16)
    for hd in range(N_HEADS):
        lo, mid, hi = hd * VAUG_W, hd * VAUG_W + LANES, (hd + 1) * VAUG_W
        mqo_ref[0, :, lo:mid] = (qm[:, lo:mid] * mla_qscale).astype(BF16)
        qr = _rope_lanes(qm[:, mid:hi], c, s1, s2) * mla_qscale
        mqo_ref[0, :, mid:hi] = qr.astype(BF16)
        mko_ref[0, :, lo:mid] = kv[:, lo:mid].astype(BF16)
        mko_ref[0, :, mid:hi] = kr
        mvo_ref[0, :, lo:mid] = kv[:, mid:hi].astype(BF16)
        mvo_ref[0, :, mid:hi] = ones_col


def _prep(proj3, tables, qnw, wqb, kvnw, wkvb):
    b, s, _ = proj3.shape
    tm = min(512, s)
    cos, s1, s2 = tables
    hw = N_HEADS * LANES
    aw = N_HEADS * VAUG_W
    tok = lambda cb: (lambda bi, i: (bi, i, cb))
    tab = pl.BlockSpec((tm, LANES), lambda bi, i: (i, 0))
    const = lambda shape: pl.BlockSpec(shape, lambda bi, i: (0, 0))
    out_sd = lambda w: jax.ShapeDtypeStruct((b, s, w), BF16)
    out_bs = lambda w: pl.BlockSpec((1, tm, w), lambda bi, i: (bi, i, 0))
    return pl.pallas_call(
        _prep_kernel,
        out_shape=(out_sd(hw), out_sd(hw), out_sd(aw), out_sd(aw), out_sd(aw), out_sd(aw)),
        grid=(b, s // tm),
        in_specs=[pl.BlockSpec((1, tm, hw), tok(COL_DIFF // hw)),
                  pl.BlockSpec((1, tm, hw), tok(COL_DIFF // hw + 1)),
                  pl.BlockSpec((1, tm, hw), tok(COL_DIFF // hw + 2)),
                  pl.BlockSpec((1, tm, hw), tok(COL_MLA // hw)),
                  pl.BlockSpec((1, tm, LANES), tok(COL_KRGL // LANES)),
                  tab, tab, tab,
                  const((1, MLA_Q_LORA)), const(wqb.shape),
                  const((1, MLA_KV_LORA)), const(wkvb.shape)],
        out_specs=(out_bs(hw), out_bs(hw), out_bs(aw), out_bs(aw), out_bs(aw), out_bs(aw)),
        compiler_params=_cparams(("parallel", "parallel"), VMEM_LIMIT),
        name="branch_prep",
    )(proj3, proj3, proj3, proj3, proj3, cos, s1, s2,
      qnw.reshape(1, -1), wqb, kvnw.reshape(1, -1), wkvb)


def _flash_kernel(*refs, n_maps, tk, lam_init):
    if n_maps == 2:
        q_ref, k_ref, v_ref, lam_ref, subw_ref, o_ref, m_scr, acc_scr, s_scr = refs
    else:
        q_ref, k_ref, v_ref, o_ref, m_scr, acc_scr, s_scr = refs
    nk = k_ref.shape[1] // tk
    assert nk >= 2 and nk % 2 == 0
    q = q_ref[0]
    if n_maps == 2:
        lane = lax.broadcasted_iota(jnp.int32, q.shape, 1)
        zero = jnp.zeros_like(q)
        qs = (jnp.where(lane < DIFF_D, q, zero), jnp.where(lane >= DIFF_D, q, zero))
    else:
        qs = (q,)

    m_scr[...] = jnp.full(m_scr.shape, -jnp.inf, F32)
    acc_scr[...] = jnp.zeros(acc_scr.shape, F32)

    def scores(j, slot):
        k = k_ref[0, pl.ds(pl.multiple_of(j * tk, tk), tk), :]
        for mi in range(n_maps):
            s_scr[slot, mi] = _nt_dot(qs[mi], k)

    def accumulate(j, slot):
        v = v_ref[0, pl.ds(pl.multiple_of(j * tk, tk), tk), :]
        for mi in range(n_maps):
            s = s_scr[slot, mi]
            m_prev = m_scr[mi]
            m_new = jnp.maximum(m_prev, jnp.max(s, axis=1, keepdims=True))
            alpha = jnp.exp2(m_prev - m_new)
            p = jnp.exp2(s - pltpu.repeat(m_new, tk // LANES, 1))
            pv = jnp.dot(p.astype(BF16), v, preferred_element_type=F32)
            acc_scr[mi] = acc_scr[mi] * pltpu.repeat(alpha, VAUG_W // LANES, 1) + pv
            m_scr[mi] = m_new

    scores(0, 0)

    def body(i, carry):
        j = 2 * i
        scores(j + 1, 1)
        accumulate(j, 0)
        scores(j + 2, 0)
        accumulate(j + 1, 1)
        return carry

    lax.fori_loop(0, nk // 2 - 1, body, 0)
    scores(nk - 1, 1)
    accumulate(nk - 2, 0)
    accumulate(nk - 1, 1)

    outs = []
    for mi in range(n_maps):
        acc = acc_scr[mi]
        outs.append(acc[:, :LANES] / acc[:, LANES:LANES + 1])
    if n_maps == 2:
        lq = lam_ref[...]
        lam = (jnp.exp(jnp.sum(lq[0:1] * lq[1:2], axis=1, keepdims=True))
               - jnp.exp(jnp.sum(lq[2:3] * lq[3:4], axis=1, keepdims=True)) + lam_init)
        o = outs[0] - lam * outs[1]
        y = o * lax.rsqrt(jnp.mean(o * o, axis=-1, keepdims=True) + EPS) * subw_ref[...]
        o = y * (1.0 - lam_init)
    else:
        o = outs[0]
    o_ref[0] = o.astype(o_ref.dtype)


def _flash(q, k, vaug, *, n_maps, lam_qk=None, subw=None, lam_init=0.0):
    b, s, qw_all = q.shape
    qw = qw_all // N_HEADS
    kw = k.shape[2] // N_HEADS
    tq = min(512, s)
    tk = min(1024, s)
    in_specs = [pl.BlockSpec((1, tq, qw), lambda bi, hd, i: (bi, i, hd)),
                pl.BlockSpec((1, s, kw), lambda bi, hd, i: (bi, 0, hd)),
                pl.BlockSpec((1, s, VAUG_W), lambda bi, hd, i: (bi, 0, hd))]
    args = [q, k, vaug]
    if n_maps == 2:
        in_specs += [pl.BlockSpec(lam_qk.shape, lambda bi, hd, i: (0, 0)),
                     pl.BlockSpec((1, LANES), lambda bi, hd, i: (0, 0))]
        args += [lam_qk, subw.reshape(1, LANES)]
    return pl.pallas_call(
        functools.partial(_flash_kernel, n_maps=n_maps, tk=tk, lam_init=lam_init),
        out_shape=jax.ShapeDtypeStruct((b, s, N_HEADS * LANES), BF16),
        grid=(b, N_HEADS, s // tq),
        in_specs=in_specs,
        out_specs=pl.BlockSpec((1, tq, LANES), lambda bi, hd, i: (bi, i, hd)),
        scratch_shapes=[pltpu.VMEM((n_maps, tq, LANES), F32),
                        pltpu.VMEM((n_maps, tq, VAUG_W), F32),
                        pltpu.VMEM((2, n_maps, tq, tk), F32)],
        compiler_params=_cparams(("parallel", "parallel", "parallel"), VMEM_LIMIT),
        name="diff_attention" if n_maps == 2 else "latent_attention",
    )(*args)


def _na_kernel(q_ref, k_ref, v_ref, b_ref, o_ref, *, rows):
    qi = pl.program_id(2)
    ks_row = jnp.clip(qi * NA_QROWS - WIN_R // 2, 0, rows - NA_KROWS)
    ks = pl.multiple_of(ks_row * GRID_W, 256)
    k = k_ref[0, pl.ds(ks, NA_TK), :]
    v = v_ref[0, pl.ds(ks, NA_TK), :]
    s = _nt_dot(q_ref[0], k) * (NA_HEAD_DIM ** -0.5) + b_ref[0, 0]
    m = jnp.max(s, axis=1, keepdims=True)
    p = jnp.exp(s - m)
    l = jnp.sum(p, axis=1, keepdims=True)
    o = jnp.dot(p.astype(BF16), v, preferred_element_type=F32) / l
    o_ref[0] = o.astype(o_ref.dtype)


def _na_bias_tiles(rpb):
    a = np.arange(NA_QROWS)
    w = np.arange(NA_KROWS)
    c = np.arange(GRID_W)
    cs = np.clip(c - WIN_C // 2, 0, GRID_W - WIN_C)
    col_ok = (c[None, :] >= cs[:, None]) & (c[None, :] < cs[:, None] + WIN_C)
    dc = c[None, :] - c[:, None] + (WIN_C - 1)
    sel_c = ((dc[..., None] == np.arange(2 * WIN_C - 1)) & col_ok[..., None]).astype(np.float32)
    half = WIN_R // 2
    sel_r, row_oks = [], []
    for off, wstart in ((0, np.maximum(a - half, 0)),
                        (half, a),
                        (NA_QROWS, np.minimum(a + half, NA_QROWS))):
        row_ok = (w[None, :] >= wstart[:, None]) & (w[None, :] < wstart[:, None] + WIN_R)
        dr = w[None, :] - a[:, None] - off + (WIN_R - 1)
        sel_r.append(((dr[..., None] == np.arange(2 * WIN_R - 1)) & row_ok[..., None]).astype(np.float32))
        row_oks.append(row_ok)
    sel_r = np.stack(sel_r)
    ok = np.stack(row_oks)[:, :, None, :, None] & col_ok[None, None, :, None, :]
    toep = jnp.einsum("qkc,lhdc->lhdqk", sel_c, rpb.astype(F32), precision=lax.Precision.HIGHEST)
    tiles = jnp.einsum("tawd,lhdqk->lthaqwk", sel_r, toep, precision=lax.Precision.HIGHEST)
    tiles = jnp.where(ok[None, :, None], tiles, NEG_BIG)
    return tiles.reshape(rpb.shape[0], 3, N_HEADS, NA_TQ, NA_TK)


def _na(proj3, bias_tiles):
    b, s, _ = proj3.shape
    rows = s // GRID_W
    assert rows % NA_QROWS == 0 and rows >= NA_KROWS
    nq = rows // NA_QROWS
    qb, kb, vb = COL_NA // LANES, COL_NA // LANES + N_HEADS, COL_NA // LANES + 2 * N_HEADS

    def bias_idx(bi, hd, i):
        kind = jnp.where(i == 0, 0, jnp.where(i == nq - 1, 2, 1))
        return (kind, hd, 0, 0)

    return pl.pallas_call(
        functools.partial(_na_kernel, rows=rows),
        out_shape=jax.ShapeDtypeStruct((b, s, N_HEADS * LANES), BF16),
        grid=(b, N_HEADS, nq),
        in_specs=[pl.BlockSpec((1, NA_TQ, LANES), lambda bi, hd, i: (bi, i, qb + hd)),
                  pl.BlockSpec((1, s, LANES), lambda bi, hd, i: (bi, 0, kb + hd)),
                  pl.BlockSpec((1, s, LANES), lambda bi, hd, i: (bi, 0, vb + hd)),
                  pl.BlockSpec((1, 1, NA_TQ, NA_TK), bias_idx)],
        out_specs=pl.BlockSpec((1, NA_TQ, LANES), lambda bi, hd, i: (bi, i, hd)),
        compiler_params=_cparams(("parallel", "parallel", "parallel"), VMEM_LIMIT),
        name="neighborhood_attention",
    )(proj3, proj3, proj3, bias_tiles)


def _gla_kernel(*refs, reverse):
    if reverse:
        q_ref, k_ref, v_ref, gl_ref, wg_ref, bg_ref, ofwd_ref, nw_ref, o_ref, st_scr = refs
    else:
        q_ref, k_ref, v_ref, gl_ref, wg_ref, bg_ref, o_ref, st_scr = refs
    t = q_ref.shape[1]
    gw = N_HEADS * GLA_DK
    c_len = GLA_CHUNK

    @pl.when(pl.program_id(1) == 0)
    def _():
        st_scr[...] = jnp.zeros(st_scr.shape, F32)

    gpre = jnp.dot(gl_ref[0], wg_ref[...], preferred_element_type=F32) + bg_ref[...]
    g = (jnp.minimum(gpre, 0.0) - jnp.log1p(jnp.exp(-jnp.abs(gpre)))) * (1.0 / GLA_TAU)

    pos = lax.broadcasted_iota(jnp.int32, (t, gw), 0) & (c_len - 1)
    cum = g
    for sh in (1, 2, 4, 8, 16, 32):
        if reverse:
            cum = cum + jnp.where(pos < c_len - sh, pltpu.roll(cum, t - sh, 0), 0.0)
        else:
            cum = cum + jnp.where(pos >= sh, pltpu.roll(cum, sh, 0), 0.0)

    qf = q_ref[0].astype(F32) * (GLA_DK ** -0.5)
    kf = k_ref[0].astype(F32)
    ref_i = c_len // 2 if reverse else c_len // 2 - 1
    last_i = 0 if reverse else c_len - 1
    ri = lax.broadcasted_iota(jnp.int32, (c_len, c_len), 0)
    ci = lax.broadcasted_iota(jnp.int32, (c_len, c_len), 1)
    tri = (ci >= ri) if reverse else (ri >= ci)
    lane = lax.broadcasted_iota(jnp.int32, (c_len, LANES), 1)
    n_chunks = t // c_len
    order = range(n_chunks - 1, -1, -1) if reverse else range(n_chunks)
    for n in order:
        r0 = n * c_len
        cn = cum[r0:r0 + c_len]
        ref = cum[r0 + ref_i:r0 + ref_i + 1]
        last = cum[r0 + last_i:r0 + last_i + 1]
        qn, kn = qf[r0:r0 + c_len], kf[r0:r0 + c_len]
        qd = qn * jnp.exp(cn - ref)
        kd = (kn * jnp.exp(ref - cn)).astype(BF16)
        qe = qn * jnp.exp(cn)
        kl = (kn * jnp.exp(last - cn)).astype(BF16)
        dec = jnp.exp(last)
        for hd in range(N_HEADS):
            lsl = slice((hd // 2) * LANES, (hd // 2 + 1) * LANES)
            own = (lane < GLA_DK) if hd % 2 == 0 else (lane >= GLA_DK)
            vsl = slice(hd * GLA_DV, (hd + 1) * GLA_DV)
            v_h = v_ref[0, r0:r0 + c_len, vsl]
            a = _nt_dot(jnp.where(own, qd[:, lsl], 0.0).astype(BF16), kd[:, lsl])
            a = jnp.where(tri, a, 0.0)
            o_h = jnp.dot(a.astype(BF16), v_h, preferred_element_type=F32)
            st = st_scr[hd]
            o_h = o_h + _nt_dot(jnp.where(own, qe[:, lsl], 0.0).astype(BF16), st.astype(BF16))
            st_scr[hd] = st * dec[:, lsl] + _tn_dot(v_h, kl[:, lsl])
            if reverse:
                tot = o_h + ofwd_ref[0, r0:r0 + c_len, vsl]
                y = tot * lax.rsqrt(jnp.mean(tot * tot, axis=-1, keepdims=True) + EPS) * nw_ref[...]
                o_ref[0, r0:r0 + c_len, vsl] = y.astype(o_ref.dtype)
            else:
                o_ref[0, r0:r0 + c_len, vsl] = o_h


def _gla_pass(proj3, wg, bg, reverse, o_fwd=None, norm_w=None):
    b, s, _ = proj3.shape
    t = min(GLA_T, s)
    nb = s // t
    gw = N_HEADS * GLA_DK
    vw = N_HEADS * GLA_DV
    blk = (lambda i: nb - 1 - i) if reverse else (lambda i: i)
    tok = lambda cb: (lambda bi, i: (bi, blk(i), cb))
    const = lambda shape: pl.BlockSpec(shape, lambda bi, i: (0, 0))
    in_specs = [pl.BlockSpec((1, t, gw), tok(COL_GLA // gw)),
                pl.BlockSpec((1, t, gw), tok(COL_GLA // gw + 1)),
                pl.BlockSpec((1, t, vw), tok((COL_GLA + 2 * gw) // vw)),
                pl.BlockSpec((1, t, LANES), tok(COL_KRGL // LANES)),
                const((LANES, gw)), const((1, gw))]
    args = [proj3, proj3, proj3, proj3, wg, bg]
    if reverse:
        in_specs += [pl.BlockSpec((1, t, vw), tok(0)), const((1, GLA_DV))]
        args += [o_fwd, norm_w.reshape(1, GLA_DV)]
    return pl.pallas_call(
        functools.partial(_gla_kernel, reverse=reverse),
        out_shape=jax.ShapeDtypeStruct((b, s, vw), BF16 if reverse else F32),
        grid=(b, nb),
        in_specs=in_specs,
        out_specs=pl.BlockSpec((1, t, vw), tok(0)),
        scratch_shapes=[pltpu.VMEM((N_HEADS, GLA_DV, LANES), F32)],
        compiler_params=_cparams(("parallel", "arbitrary")),
        name="gla_backward" if reverse else "gla_forward",
    )(*args)


def _gla(proj3, w_gate_up, b_gate, norm_w):
    gw = N_HEADS * GLA_DK
    wgs = []
    for d in range(2):
        lo = GL_LANE0 + d * GLA_RANK
        wgs.append(jnp.zeros((LANES, gw), F32).at[lo:lo + GLA_RANK].set(w_gate_up[d]).astype(BF16))
    o_fwd = _gla_pass(proj3, wgs[0], b_gate[0].reshape(1, gw), False)
    return _gla_pass(proj3, wgs[1], b_gate[1].reshape(1, gw), True, o_fwd, norm_w)


def _merge_kernel(h_ref, oa_ref, ob_ref, oc_ref, od_ref, za_ref, zb_ref, zc_ref, zd_ref,
                  wg_ref, wb_ref, o_ref):
    h = h_ref[...]
    acc = None
    for i, (o_r, z_r) in enumerate(((oa_ref, za_ref), (ob_ref, zb_ref), (oc_ref, zc_ref), (od_ref, zd_ref))):
        z = z_r[...].astype(F32)
        u = (o_r[...].astype(F32) * (z * jax.nn.sigmoid(z))).astype(BF16)
        t = jnp.dot(u, wb_ref[i], preferred_element_type=F32)
        gate = jax.nn.sigmoid(jnp.dot(h, wg_ref[i], preferred_element_type=F32))
        acc = gate * t if acc is None else acc + gate * t
    o_ref[...] = acc.astype(o_ref.dtype)


def _merge(h, branches, proj, wg, wb):
    n, d = h.shape
    tm = min(512, n)
    tn = min(512, d)
    zb0 = COL_Z // BRANCH_W
    tok = pl.BlockSpec((tm, BRANCH_W), lambda j, i: (i, 0))
    zspec = lambda k: pl.BlockSpec((tm, BRANCH_W), lambda j, i: (i, zb0 + k))
    return pl.pallas_call(
        _merge_kernel,
        out_shape=jax.ShapeDtypeStruct((n, d), BF16),
        grid=(d // tn, n // tm),
        in_specs=[pl.BlockSpec((tm, d), lambda j, i: (i, 0)), tok, tok, tok, tok,
                  zspec(0), zspec(1), zspec(2), zspec(3),
                  pl.BlockSpec((N_BRANCH, d, tn), lambda j, i: (0, 0, j)),
                  pl.BlockSpec((N_BRANCH, BRANCH_W, tn), lambda j, i: (0, 0, j))],
        out_specs=pl.BlockSpec((tm, tn), lambda j, i: (i, j)),
        compiler_params=_cparams(("parallel", "parallel"), VMEM_LIMIT),
        name="branch_merge",
    )(h, *branches, proj, proj, proj, proj, wg, wb)


def _out_kernel(m_ref, w_ref, x_ref, pw_ref, nw_ref, xo_ref, ho_ref):
    y = jnp.dot(m_ref[...], w_ref[...], preferred_element_type=F32)
    y = y * lax.rsqrt(jnp.mean(y * y, axis=-1, keepdims=True) + EPS) * pw_ref[...]
    xn = x_ref[...] + y
    xo_ref[...] = xn
    hn = xn * lax.rsqrt(jnp.mean(xn * xn, axis=-1, keepdims=True) + EPS) * nw_ref[...]
    ho_ref[...] = hn.astype(ho_ref.dtype)


def _out_proj(merged, w_out, x, post_w, next_pre_w):
    n, d = x.shape
    tm = min(256, n)
    tok = lambda: pl.BlockSpec((tm, d), lambda i: (i, 0))
    vec = lambda: pl.BlockSpec((1, d), lambda i: (0, 0))
    return pl.pallas_call(
        _out_kernel,
        out_shape=(jax.ShapeDtypeStruct((n, d), F32), jax.ShapeDtypeStruct((n, d), BF16)),
        grid=(n // tm,),
        in_specs=[tok(), pl.BlockSpec((d, d), lambda i: (0, 0)), tok(), vec(), vec()],
        out_specs=(tok(), tok()),
        compiler_params=_cparams(("parallel",), VMEM_LIMIT),
        name="out_proj",
    )(merged, w_out, x, post_w.reshape(1, d), next_pre_w.reshape(1, d))


def _permute_w_in(w_in):
    depth, d, _ = w_in.shape
    na = w_in[:, :, :NA_COLS]
    diff = w_in[:, :, NA_COLS:NA_COLS + DIFF_COLS]
    g0 = NA_COLS + DIFF_COLS
    gla_qkv = w_in[:, :, g0:g0 + GLA_QKV_COLS]
    gl = w_in[:, :, g0 + GLA_QKV_COLS:g0 + GLA_COLS]
    m0 = g0 + GLA_COLS
    cq_ckv = w_in[:, :, m0:m0 + MLA_Q_LORA + MLA_KV_LORA]
    kr = w_in[:, :, m0 + MLA_Q_LORA + MLA_KV_LORA:m0 + MLA_COLS]
    z = w_in[:, :, m0 + MLA_COLS:]
    used = COL_KRGL + MLA_ROPE + 2 * GLA_RANK
    pad = jnp.zeros((depth, d, PROJ_COLS - used), w_in.dtype)
    return jnp.concatenate([na, diff, gla_qkv, cq_ckv, z, kr, gl, pad], axis=-1).astype(BF16)


def _pad_w_qb(w_qb):
    depth, r, _ = w_qb.shape
    w = w_qb.reshape(depth, r, N_HEADS, MLA_NOPE + MLA_ROPE)
    w = jnp.pad(w, ((0, 0), (0, 0), (0, 0), (0, VAUG_W - MLA_NOPE - MLA_ROPE)))
    return w.reshape(depth, r, N_HEADS * VAUG_W).astype(BF16)


def _rope_tables(s):
    half = DIFF_D // 2
    inv = ROPE_THETA ** (-jnp.arange(0, DIFF_D, 2, dtype=F32) / DIFF_D)
    ang = jnp.arange(s, dtype=F32)[:, None] * inv[None, :]
    cos, sin = jnp.cos(ang), jnp.sin(ang)
    zero = jnp.zeros_like(sin)
    reps = LANES // DIFF_D
    c = jnp.tile(jnp.concatenate([cos, cos], axis=1), (1, reps))
    s1 = jnp.tile(jnp.concatenate([-sin, zero], axis=1), (1, reps))
    s2 = jnp.tile(jnp.concatenate([zero, sin], axis=1), (1, reps))
    assert half * 2 * reps == LANES
    return c, s1, s2


def _trunk(x, p):
    b, s, d = x.shape
    n = b * s
    tables = _rope_tables(s)
    xf = x.reshape(n, d)
    h = _rmsnorm(xf, p["pre_norm_w"][0])
    for l in range(DEPTH):
        lam_init = 0.8 - 0.6 * math.exp(-0.3 * l)
        proj = _in_proj(h, p["w_in"][l])
        proj3 = proj.reshape(b, s, PROJ_COLS)
        dq, dk, dva, mq, mk, mva = _prep(proj3, tables, p["mla_q_norm_w"][l], p["mla_w_qb"][l],
                                         p["mla_kv_norm_w"][l], p["mla_w_kvb"][l])
        o_a = _na(proj3, p["na_bias"][l])
        o_b = _flash(dq, dk, dva, n_maps=2, lam_qk=p["diff_lambda_qk"][l],
                     subw=p["diff_subln_w"][l], lam_init=lam_init)
        o_c = _gla(proj3, p["gla_w_gate_up"][l], p["gla_b_gate"][l], p["gla_norm_w"][l])
        o_d = _flash(mq, mk, mva, n_maps=1)
        branches = [o.reshape(n, BRANCH_W) for o in (o_a, o_b, o_c, o_d)]
        merged = _merge(h, branches, proj, p["w_gate"][l], p["w_branch"][l])
        next_pre = p["pre_norm_w"][(l + 1) % DEPTH]
        xf, h = _out_proj(merged, p["w_out"][l], xf, p["post_norm_w"][l], next_pre)
    return xf.reshape(b, s, d)


def kernel(x_prompt, x_sample, pre_norm_w, w_in, na_rpb, diff_lambda_qk, diff_subln_w, gla_w_gate_up, gla_b_gate, gla_norm_w, mla_q_norm_w, mla_w_qb, mla_kv_norm_w, mla_w_kvb, w_gate, w_branch, w_out, post_norm_w):
    p = {
        "pre_norm_w": pre_norm_w,
        "w_in": _permute_w_in(w_in),
        "na_bias": _na_bias_tiles(na_rpb),
        "diff_lambda_qk": diff_lambda_qk,
        "diff_subln_w": diff_subln_w,
        "gla_w_gate_up": gla_w_gate_up,
        "gla_b_gate": gla_b_gate,
        "gla_norm_w": gla_norm_w,
        "mla_q_norm_w": mla_q_norm_w,
        "mla_w_qb": _pad_w_qb(mla_w_qb),
        "mla_kv_norm_w": mla_kv_norm_w,
        "mla_w_kvb": mla_w_kvb.astype(BF16),
        "w_gate": w_gate.astype(BF16),
        "w_branch": w_branch.astype(BF16),
        "w_out": w_out.astype(BF16),
        "post_norm_w": post_norm_w,
    }
    return (_trunk(x_prompt, p), _trunk(x_sample, p))
```

```python
import functools
import math

import jax
import jax.numpy as jnp
import numpy as np
from jax import lax
from jax.experimental import pallas as pl
from jax.experimental.pallas import tpu as pltpu

F32 = jnp.float32
BF16 = jnp.bfloat16

DEPTH = 4
GRID_W = 64
N_BRANCH = 4
BRANCH_W = 512
N_HEADS = 4
NA_HEAD_DIM = 128
WIN_R = 8
WIN_C = 16
DIFF_D = 64
GLA_DK = 64
GLA_DV = 128
GLA_RANK = 16
GLA_TAU = 16.0
GLA_CHUNK = 64
MLA_Q_LORA = 384
MLA_KV_LORA = 128
MLA_NOPE = 128
MLA_ROPE = 64
MLA_V = 128
ROPE_THETA = 10000.0
EPS = 1e-6
LOG2E = math.log2(math.e)

NA_COLS = 3 * N_HEADS * NA_HEAD_DIM
DIFF_COLS = 3 * N_HEADS * 2 * DIFF_D
GLA_QKV_COLS = 2 * N_HEADS * GLA_DK + N_HEADS * GLA_DV
GLA_COLS = GLA_QKV_COLS + 2 * GLA_RANK
MLA_COLS = MLA_Q_LORA + MLA_KV_LORA + MLA_ROPE
Z_COLS = N_BRANCH * BRANCH_W

LANES = 128
V7X_VMEM_BYTES = 64 * 1024 * 1024
VMEM_LIMIT = 56 * 1024 * 1024
MIB = 1024 * 1024

COL_NA = 0
COL_DIFF = COL_NA + NA_COLS
COL_GLA = COL_DIFF + DIFF_COLS
COL_MLA = COL_GLA + GLA_QKV_COLS
COL_Z = COL_MLA + MLA_Q_LORA + MLA_KV_LORA
COL_KRGL = COL_Z + Z_COLS
PROJ_TN = 1152
PROJ_COLS = 6 * PROJ_TN
GL_LANE0 = MLA_ROPE

NA_QROWS = 8
NA_KROWS = 16
NA_TQ = NA_QROWS * GRID_W
NA_TK = NA_KROWS * GRID_W
NA_KV_PARTS = 4
NEG_BIG = -1e30

VAUG_W = 2 * LANES
FLASH_TQ = 1024
FLASH_TK = 1024
GLA_T = 256


def _cparams(sem, vmem=None):
    return pltpu.CompilerParams(dimension_semantics=sem, vmem_limit_bytes=vmem)


def _nt_dot(a, b):
    return lax.dot_general(a, b, (((1,), (1,)), ((), ())), preferred_element_type=F32)


def _tn_dot(a, b):
    return lax.dot_general(a, b, (((0,), (0,)), ((), ())), preferred_element_type=F32)


def _rmsnorm_kernel(x_ref, w_ref, o_ref):
    x = x_ref[...]
    y = x * lax.rsqrt(jnp.mean(x * x, axis=-1, keepdims=True) + EPS)
    o_ref[...] = (y * w_ref[...]).astype(o_ref.dtype)


def _rmsnorm(x, w):
    n, d = x.shape
    tm = min(512, n)
    return pl.pallas_call(
        _rmsnorm_kernel,
        out_shape=jax.ShapeDtypeStruct((n, d), BF16),
        grid=(n // tm,),
        in_specs=[pl.BlockSpec((tm, d), lambda i: (i, 0)),
                  pl.BlockSpec((1, d), lambda i: (0, 0))],
        out_specs=pl.BlockSpec((tm, d), lambda i: (i, 0)),
        compiler_params=_cparams(("parallel",)),
        name="pre_norm",
    )(x, w.reshape(1, d))


def _matmul_kernel(a_ref, b_ref, o_ref):
    o_ref[...] = jnp.dot(a_ref[...], b_ref[...], preferred_element_type=F32).astype(o_ref.dtype)


def _in_proj(h, w):
    n, d = h.shape
    c = w.shape[1]
    tm = min(1024, n)
    tn = PROJ_TN
    return pl.pallas_call(
        _matmul_kernel,
        out_shape=jax.ShapeDtypeStruct((n, c), BF16),
        grid=(n // tm, c // tn),
        in_specs=[pl.BlockSpec((tm, d), lambda i, j: (i, 0)),
                  pl.BlockSpec((d, tn), lambda i, j: (0, j))],
        out_specs=pl.BlockSpec((tm, tn), lambda i, j: (i, j)),
        compiler_params=_cparams(("parallel", "parallel"), VMEM_LIMIT),
        name="in_proj",
    )(h, w)


def _rope_lanes(x, c, s1, s2):
    return x * c + pltpu.roll(x, LANES - 32, 1) * s1 + pltpu.roll(x, 32, 1) * s2


def _prep_kernel(dq_ref, dk_ref, dv_ref, cqkv_ref, krgl_ref, cos_ref, s1_ref, s2_ref,
                 qnw_ref, wqb_ref, kvnw_ref, wkvb_ref,
                 dqo_ref, dko_ref, dvo_ref, mqo_ref, mko_ref, mvo_ref):
    c, s1, s2 = cos_ref[...], s1_ref[...], s2_ref[...]
    t = c.shape[0]
    lane = lax.broadcasted_iota(jnp.int32, (t, LANES), 1)
    ones_col = jnp.where(lane == 0, 1.0, 0.0).astype(BF16)
    diff_qscale = (DIFF_D ** -0.5) * LOG2E
    mla_qscale = ((MLA_NOPE + MLA_ROPE) ** -0.5) * LOG2E

    for hd in range(N_HEADS):
        sl = slice(hd * LANES, (hd + 1) * LANES)
        q = _rope_lanes(dq_ref[0, :, sl].astype(F32), c, s1, s2) * diff_qscale
        dqo_ref[0, :, sl] = q.astype(BF16)
        k = _rope_lanes(dk_ref[0, :, sl].astype(F32), c, s1, s2)
        dko_ref[0, :, sl] = k.astype(BF16)
        dvo_ref[0, :, hd * VAUG_W:hd * VAUG_W + LANES] = dv_ref[0, :, sl]
        dvo_ref[0, :, hd * VAUG_W + LANES:(hd + 1) * VAUG_W] = ones_col

    cq = cqkv_ref[0, :, :MLA_Q_LORA].astype(F32)
    cqn = cq * lax.rsqrt(jnp.mean(cq * cq, axis=-1, keepdims=True) + EPS) * qnw_ref[...]
    qm = jnp.dot(cqn.astype(BF16), wqb_ref[...], preferred_element_type=F32)
    ckv = cqkv_ref[0, :, MLA_Q_LORA:].astype(F32)
    ckvn = ckv * lax.rsqrt(jnp.mean(ckv * ckv, axis=-1, keepdims=True) + EPS) * kvnw_ref[...]
    kv = jnp.dot(ckvn.astype(BF16), wkvb_ref[...], preferred_element_type=F32)
    kr = jnp.where(lane < MLA_ROPE, krgl_ref[0].astype(F32), 0.0)
    kr = _rope_lanes(kr, c, s1, s2).astype(BF16)
    for hd in range(N_HEADS):
        lo, mid, hi = hd * VAUG_W, hd * VAUG_W + LANES, (hd + 1) * VAUG_W
        mqo_ref[0, :, lo:mid] = (qm[:, lo:mid] * mla_qscale).astype(BF16)
        qr = _rope_lanes(qm[:, mid:hi], c, s1, s2) * mla_qscale
        mqo_ref[0, :, mid:hi] = qr.astype(BF16)
        mko_ref[0, :, lo:mid] = kv[:, lo:mid].astype(BF16)
        mko_ref[0, :, mid:hi] = kr
        mvo_ref[0, :, lo:mid] = kv[:, mid:hi].astype(BF16)
        mvo_ref[0, :, mid:hi] = ones_col


def _prep(proj3, tables, qnw, wqb, kvnw, wkvb):
    b, s, _ = proj3.shape
    tm = min(512, s)
    cos, s1, s2 = tables
    hw = N_HEADS * LANES
    aw = N_HEADS * VAUG_W
    tok = lambda cb: (lambda bi, i: (bi, i, cb))
    tab = pl.BlockSpec((tm, LANES), lambda bi, i: (i, 0))
    const = lambda shape: pl.BlockSpec(shape, lambda bi, i: (0, 0))
    out_sd = lambda w: jax.ShapeDtypeStruct((b, s, w), BF16)
    out_bs = lambda w: pl.BlockSpec((1, tm, w), lambda bi, i: (bi, i, 0))
    return pl.pallas_call(
        _prep_kernel,
        out_shape=(out_sd(hw), out_sd(hw), out_sd(aw), out_sd(aw), out_sd(aw), out_sd(aw)),
        grid=(b, s // tm),
        in_specs=[pl.BlockSpec((1, tm, hw), tok(COL_DIFF // hw)),
                  pl.BlockSpec((1, tm, hw), tok(COL_DIFF // hw + 1)),
                  pl.BlockSpec((1, tm, hw), tok(COL_DIFF // hw + 2)),
                  pl.BlockSpec((1, tm, hw), tok(COL_MLA // hw)),
                  pl.BlockSpec((1, tm, LANES), tok(COL_KRGL // LANES)),
                  tab, tab, tab,
                  const((1, MLA_Q_LORA)), const(wqb.shape),
                  const((1, MLA_KV_LORA)), const(wkvb.shape)],
        out_specs=(out_bs(hw), out_bs(hw), out_bs(aw), out_bs(aw), out_bs(aw), out_bs(aw)),
        compiler_params=_cparams(("parallel", "parallel"), VMEM_LIMIT),
        name="branch_prep",
    )(proj3, proj3, proj3, proj3, proj3, cos, s1, s2,
      qnw.reshape(1, -1), wqb, kvnw.reshape(1, -1), wkvb)


def _flash_kernel(*refs, n_maps, tk, lam_init):
    if n_maps == 2:
        q_ref, k_ref, v_ref, lam_ref, subw_ref, o_ref, m_scr, acc_scr, s_scr = refs
    else:
        q_ref, k_ref, v_ref, o_ref, m_scr, acc_scr, s_scr = refs
    nk = k_ref.shape[1] // tk
    assert nk >= 2 and nk % 2 == 0
    q = q_ref[0]
    if n_maps == 2:
        lane = lax.broadcasted_iota(jnp.int32, q.shape, 1)
        zero = jnp.zeros_like(q)
        qs = (jnp.where(lane < DIFF_D, q, zero), jnp.where(lane >= DIFF_D, q, zero))
    else:
        qs = (q,)

    m_scr[...] = jnp.full(m_scr.shape, -jnp.inf, F32)
    acc_scr[...] = jnp.zeros(acc_scr.shape, F32)

    def scores(j, slot):
        k = k_ref[0, pl.ds(pl.multiple_of(j * tk, tk), tk), :]
        for mi in range(n_maps):
            s_scr[slot, mi] = _nt_dot(qs[mi], k)

    def accumulate(j, slot):
        v = v_ref[0, pl.ds(pl.multiple_of(j * tk, tk), tk), :]
        for mi in range(n_maps):
            s = s_scr[slot, mi]
            m_prev = m_scr[mi]
            m_new = jnp.maximum(m_prev, jnp.max(s, axis=1, keepdims=True))
            alpha = jnp.exp2(m_prev - m_new)
            p = jnp.exp2(s - jnp.tile(m_new, (1, tk // LANES)))
            pv = jnp.dot(p.astype(BF16), v, preferred_element_type=F32)
            acc_scr[mi] = acc_scr[mi] * jnp.tile(alpha, (1, VAUG_W // LANES)) + pv
            m_scr[mi] = m_new

    scores(0, 0)

    def body(i, carry):
        j = 2 * i
        scores(j + 1, 1)
        accumulate(j, 0)
        scores(j + 2, 0)
        accumulate(j + 1, 1)
        return carry

    lax.fori_loop(0, nk // 2 - 1, body, 0)
    scores(nk - 1, 1)
    accumulate(nk - 2, 0)
    accumulate(nk - 1, 1)

    outs = []
    for mi in range(n_maps):
        acc = acc_scr[mi]
        outs.append(acc[:, :LANES] / acc[:, LANES:LANES + 1])
    if n_maps == 2:
        lq = lam_ref[...]
        lam = (jnp.exp(jnp.sum(lq[0:1] * lq[1:2], axis=1, keepdims=True))
               - jnp.exp(jnp.sum(lq[2:3] * lq[3:4], axis=1, keepdims=True)) + lam_init)
        o = outs[0] - lam * outs[1]
        y = o * lax.rsqrt(jnp.mean(o * o, axis=-1, keepdims=True) + EPS) * subw_ref[...]
        o = y * (1.0 - lam_init)
    else:
        o = outs[0]
    o_ref[0] = o.astype(o_ref.dtype)


def _flash(q, k, vaug, *, n_maps, lam_qk=None, subw=None, lam_init=0.0):
    b, s, qw_all = q.shape
    qw = qw_all // N_HEADS
    kw = k.shape[2] // N_HEADS
    tq = min(FLASH_TQ, s)
    tk = min(FLASH_TK, s // 2)
    in_specs = [pl.BlockSpec((1, tq, qw), lambda bi, hd, i: (bi, i, hd)),
                pl.BlockSpec((1, s, kw), lambda bi, hd, i: (bi, 0, hd)),
                pl.BlockSpec((1, s, VAUG_W), lambda bi, hd, i: (bi, 0, hd))]
    args = [q, k, vaug]
    if n_maps == 2:
        in_specs += [pl.BlockSpec(lam_qk.shape, lambda bi, hd, i: (0, 0)),
                     pl.BlockSpec((1, LANES), lambda bi, hd, i: (0, 0))]
        args += [lam_qk, subw.reshape(1, LANES)]
    return pl.pallas_call(
        functools.partial(_flash_kernel, n_maps=n_maps, tk=tk, lam_init=lam_init),
        out_shape=jax.ShapeDtypeStruct((b, s, N_HEADS * LANES), BF16),
        grid=(b, N_HEADS, s // tq),
        in_specs=in_specs,
        out_specs=pl.BlockSpec((1, tq, LANES), lambda bi, hd, i: (bi, i, hd)),
        scratch_shapes=[pltpu.VMEM((n_maps, tq, LANES), F32),
                        pltpu.VMEM((n_maps, tq, VAUG_W), F32),
                        pltpu.VMEM((2, n_maps, tq, tk), F32)],
        compiler_params=_cparams(("parallel", "parallel", "parallel"), VMEM_LIMIT),
        name="diff_attention" if n_maps == 2 else "latent_attention",
    )(*args)


def _na_kernel(q_ref, *refs):
    k_refs, v_refs = refs[:NA_KV_PARTS], refs[NA_KV_PARTS:2 * NA_KV_PARTS]
    b_ref, o_ref = refs[2 * NA_KV_PARTS:]
    for hd in range(N_HEADS):
        sl = slice(hd * LANES, (hd + 1) * LANES)
        k = jnp.concatenate([r[0, :, sl] for r in k_refs], axis=0)
        v = jnp.concatenate([r[0, :, sl] for r in v_refs], axis=0)
        s = _nt_dot(q_ref[0, :, sl], k) * (NA_HEAD_DIM ** -0.5) + b_ref[0, hd]
        m = jnp.max(s, axis=1, keepdims=True)
        p = jnp.exp(s - m)
        l = jnp.sum(p, axis=1, keepdims=True)
        o = jnp.dot(p.astype(BF16), v, preferred_element_type=F32) / l
        o_ref[0, :, sl] = o.astype(o_ref.dtype)


def _na_bias_tiles(rpb):
    a = np.arange(NA_QROWS)
    w = np.arange(NA_KROWS)
    c = np.arange(GRID_W)
    cs = np.clip(c - WIN_C // 2, 0, GRID_W - WIN_C)
    col_ok = (c[None, :] >= cs[:, None]) & (c[None, :] < cs[:, None] + WIN_C)
    dc = c[None, :] - c[:, None] + (WIN_C - 1)
    sel_c = ((dc[..., None] == np.arange(2 * WIN_C - 1)) & col_ok[..., None]).astype(np.float32)
    half = WIN_R // 2
    sel_r, row_oks = [], []
    for off, wstart in ((0, np.maximum(a - half, 0)),
                        (half, a),
                        (NA_QROWS, np.minimum(a + half, NA_QROWS))):
        row_ok = (w[None, :] >= wstart[:, None]) & (w[None, :] < wstart[:, None] + WIN_R)
        dr = w[None, :] - a[:, None] - off + (WIN_R - 1)
        sel_r.append(((dr[..., None] == np.arange(2 * WIN_R - 1)) & row_ok[..., None]).astype(np.float32))
        row_oks.append(row_ok)
    sel_r = np.stack(sel_r)
    ok = np.stack(row_oks)[:, :, None, :, None] & col_ok[None, None, :, None, :]
    toep = jnp.einsum("qkc,lhdc->lhdqk", sel_c, rpb.astype(F32), precision=lax.Precision.HIGHEST)
    tiles = jnp.einsum("tawd,lhdqk->lthaqwk", sel_r, toep, precision=lax.Precision.HIGHEST)
    tiles = jnp.where(ok[None, :, None], tiles, NEG_BIG)
    return tiles.reshape(rpb.shape[0], 3, N_HEADS, NA_TQ, NA_TK)


def _na(proj3, bias_tiles):
    b, s, _ = proj3.shape
    rows = s // GRID_W
    assert rows % NA_QROWS == 0 and rows >= NA_KROWS
    nq = rows // NA_QROWS
    hw = N_HEADS * LANES
    part_rows = NA_KROWS // NA_KV_PARTS
    part = part_rows * GRID_W
    qb = COL_NA // hw

    def kv_spec(cb, t):
        def idx(bi, i):
            ks_row = jnp.clip(i * NA_QROWS - WIN_R // 2, 0, rows - NA_KROWS)
            return (bi, ks_row // part_rows + t, cb)
        return pl.BlockSpec((1, part, hw), idx)

    def bias_idx(bi, i):
        kind = jnp.where(i == 0, 0, jnp.where(i == nq - 1, 2, 1))
        return (kind, 0, 0, 0)

    return pl.pallas_call(
        _na_kernel,
        out_shape=jax.ShapeDtypeStruct((b, s, hw), BF16),
        grid=(b, nq),
        in_specs=([pl.BlockSpec((1, NA_TQ, hw), lambda bi, i: (bi, i, qb))]
                  + [kv_spec(qb + 1, t) for t in range(NA_KV_PARTS)]
                  + [kv_spec(qb + 2, t) for t in range(NA_KV_PARTS)]
                  + [pl.BlockSpec((1, N_HEADS, NA_TQ, NA_TK), bias_idx)]),
        out_specs=pl.BlockSpec((1, NA_TQ, hw), lambda bi, i: (bi, i, 0)),
        compiler_params=_cparams(("parallel", "parallel"), VMEM_LIMIT),
        name="neighborhood_attention",
    )(proj3, *([proj3] * (2 * NA_KV_PARTS)), bias_tiles)


def _gla_kernel(*refs, reverse):
    if reverse:
        q_ref, k_ref, v_ref, gl_ref, wg_ref, bg_ref, ofwd_ref, nw_ref, o_ref, st_scr = refs
    else:
        q_ref, k_ref, v_ref, gl_ref, wg_ref, bg_ref, o_ref, st_scr = refs
    t = q_ref.shape[1]
    gw = N_HEADS * GLA_DK
    c_len = GLA_CHUNK

    @pl.when(pl.program_id(1) == 0)
    def _():
        st_scr[...] = jnp.zeros(st_scr.shape, F32)

    gpre = jnp.dot(gl_ref[0], wg_ref[...], preferred_element_type=F32) + bg_ref[...]
    g = (jnp.minimum(gpre, 0.0) - jnp.log1p(jnp.exp(-jnp.abs(gpre)))) * (1.0 / GLA_TAU)

    pos = lax.broadcasted_iota(jnp.int32, (t, gw), 0) & (c_len - 1)
    cum = g
    for sh in (1, 2, 4, 8, 16, 32):
        if reverse:
            cum = cum + jnp.where(pos < c_len - sh, pltpu.roll(cum, t - sh, 0), 0.0)
        else:
            cum = cum + jnp.where(pos >= sh, pltpu.roll(cum, sh, 0), 0.0)

    qf = q_ref[0].astype(F32) * (GLA_DK ** -0.5)
    kf = k_ref[0].astype(F32)
    ref_i = c_len // 2 if reverse else c_len // 2 - 1
    last_i = 0 if reverse else c_len - 1
    ri = lax.broadcasted_iota(jnp.int32, (c_len, c_len), 0)
    ci = lax.broadcasted_iota(jnp.int32, (c_len, c_len), 1)
    tri = (ci >= ri) if reverse else (ri >= ci)
    lane = lax.broadcasted_iota(jnp.int32, (c_len, LANES), 1)
    n_chunks = t // c_len
    order = range(n_chunks - 1, -1, -1) if reverse else range(n_chunks)
    for n in order:
        r0 = n * c_len
        cn = cum[r0:r0 + c_len]
        ref = cum[r0 + ref_i:r0 + ref_i + 1]
        last = cum[r0 + last_i:r0 + last_i + 1]
        qn, kn = qf[r0:r0 + c_len], kf[r0:r0 + c_len]
        qd = qn * jnp.exp(cn - ref)
        kd = (kn * jnp.exp(ref - cn)).astype(BF16)
        qe = qn * jnp.exp(cn)
        kl = (kn * jnp.exp(last - cn)).astype(BF16)
        dec = jnp.exp(last)
        for hd in range(N_HEADS):
            lsl = slice((hd // 2) * LANES, (hd // 2 + 1) * LANES)
            own = (lane < GLA_DK) if hd % 2 == 0 else (lane >= GLA_DK)
            vsl = slice(hd * GLA_DV, (hd + 1) * GLA_DV)
            v_h = v_ref[0, r0:r0 + c_len, vsl]
            a = _nt_dot(jnp.where(own, qd[:, lsl], 0.0).astype(BF16), kd[:, lsl])
            a = jnp.where(tri, a, 0.0)
            o_h = jnp.dot(a.astype(BF16), v_h, preferred_element_type=F32)
            st = st_scr[hd]
            o_h = o_h + _nt_dot(jnp.where(own, qe[:, lsl], 0.0).astype(BF16), st.astype(BF16))
            st_scr[hd] = st * dec[:, lsl] + _tn_dot(v_h, kl[:, lsl])
            if reverse:
                tot = o_h + ofwd_ref[0, r0:r0 + c_len, vsl]
                y = tot * lax.rsqrt(jnp.mean(tot * tot, axis=-1, keepdims=True) + EPS) * nw_ref[...]
                o_ref[0, r0:r0 + c_len, vsl] = y.astype(o_ref.dtype)
            else:
                o_ref[0, r0:r0 + c_len, vsl] = o_h


def _gla_pass(proj3, wg, bg, reverse, o_fwd=None, norm_w=None):
    b, s, _ = proj3.shape
    t = min(GLA_T, s)
    nb = s // t
    gw = N_HEADS * GLA_DK
    vw = N_HEADS * GLA_DV
    blk = (lambda i: nb - 1 - i) if reverse else (lambda i: i)
    tok = lambda cb: (lambda bi, i: (bi, blk(i), cb))
    const = lambda shape: pl.BlockSpec(shape, lambda bi, i: (0, 0))
    in_specs = [pl.BlockSpec((1, t, gw), tok(COL_GLA // gw)),
                pl.BlockSpec((1, t, gw), tok(COL_GLA // gw + 1)),
                pl.BlockSpec((1, t, vw), tok((COL_GLA + 2 * gw) // vw)),
                pl.BlockSpec((1, t, LANES), tok(COL_KRGL // LANES)),
                const((LANES, gw)), const((1, gw))]
    args = [proj3, proj3, proj3, proj3, wg, bg]
    if reverse:
        in_specs += [pl.BlockSpec((1, t, vw), tok(0)), const((1, GLA_DV))]
        args += [o_fwd, norm_w.reshape(1, GLA_DV)]
    return pl.pallas_call(
        functools.partial(_gla_kernel, reverse=reverse),
        out_shape=jax.ShapeDtypeStruct((b, s, vw), BF16 if reverse else F32),
        grid=(b, nb),
        in_specs=in_specs,
        out_specs=pl.BlockSpec((1, t, vw), tok(0)),
        scratch_shapes=[pltpu.VMEM((N_HEADS, GLA_DV, LANES), F32)],
        compiler_params=_cparams(("parallel", "arbitrary")),
        name="gla_backward" if reverse else "gla_forward",
    )(*args)


def _gla(proj3, w_gate_up, b_gate, norm_w):
    gw = N_HEADS * GLA_DK
    wgs = []
    for d in range(2):
        lo = GL_LANE0 + d * GLA_RANK
        wgs.append(jnp.zeros((LANES, gw), F32).at[lo:lo + GLA_RANK].set(w_gate_up[d]).astype(BF16))
    o_fwd = _gla_pass(proj3, wgs[0], b_gate[0].reshape(1, gw), False)
    return _gla_pass(proj3, wgs[1], b_gate[1].reshape(1, gw), True, o_fwd, norm_w)


def _merge_kernel(h_ref, oa_ref, ob_ref, oc_ref, od_ref, za_ref, zb_ref, zc_ref, zd_ref,
                  wg_ref, wb_ref, o_ref):
    h = h_ref[...]
    acc = None
    for i, (o_r, z_r) in enumerate(((oa_ref, za_ref), (ob_ref, zb_ref), (oc_ref, zc_ref), (od_ref, zd_ref))):
        z = z_r[...].astype(F32)
        u = (o_r[...].astype(F32) * (z * jax.nn.sigmoid(z))).astype(BF16)
        t = jnp.dot(u, wb_ref[i], preferred_element_type=F32)
        gate = jax.nn.sigmoid(jnp.dot(h, wg_ref[i], preferred_element_type=F32))
        acc = gate * t if acc is None else acc + gate * t
    o_ref[...] = acc.astype(o_ref.dtype)


def _merge(h, branches, proj, wg, wb):
    n, d = h.shape
    tm = min(512, n)
    tn = min(512, d)
    zb0 = COL_Z // BRANCH_W
    tok = pl.BlockSpec((tm, BRANCH_W), lambda j, i: (i, 0))
    zspec = lambda k: pl.BlockSpec((tm, BRANCH_W), lambda j, i: (i, zb0 + k))
    return pl.pallas_call(
        _merge_kernel,
        out_shape=jax.ShapeDtypeStruct((n, d), BF16),
        grid=(d // tn, n // tm),
        in_specs=[pl.BlockSpec((tm, d), lambda j, i: (i, 0)), tok, tok, tok, tok,
                  zspec(0), zspec(1), zspec(2), zspec(3),
                  pl.BlockSpec((N_BRANCH, d, tn), lambda j, i: (0, 0, j)),
                  pl.BlockSpec((N_BRANCH, BRANCH_W, tn), lambda j, i: (0, 0, j))],
        out_specs=pl.BlockSpec((tm, tn), lambda j, i: (i, j)),
        compiler_params=_cparams(("parallel", "parallel"), VMEM_LIMIT),
        name="branch_merge",
    )(h, *branches, proj, proj, proj, proj, wg, wb)


def _out_kernel(m_ref, w_ref, x_ref, pw_ref, nw_ref, xo_ref, ho_ref):
    y = jnp.dot(m_ref[...], w_ref[...], preferred_element_type=F32)
    y = y * lax.rsqrt(jnp.mean(y * y, axis=-1, keepdims=True) + EPS) * pw_ref[...]
    xn = x_ref[...] + y
    xo_ref[...] = xn
    hn = xn * lax.rsqrt(jnp.mean(xn * xn, axis=-1, keepdims=True) + EPS) * nw_ref[...]
    ho_ref[...] = hn.astype(ho_ref.dtype)


def _out_proj(merged, w_out, x, post_w, next_pre_w):
    n, d = x.shape
    tm = min(512, n)
    tok = lambda: pl.BlockSpec((tm, d), lambda i: (i, 0))
    vec = lambda: pl.BlockSpec((1, d), lambda i: (0, 0))
    return pl.pallas_call(
        _out_kernel,
        out_shape=(jax.ShapeDtypeStruct((n, d), F32), jax.ShapeDtypeStruct((n, d), BF16)),
        grid=(n // tm,),
        in_specs=[tok(), pl.BlockSpec((d, d), lambda i: (0, 0)), tok(), vec(), vec()],
        out_specs=(tok(), tok()),
        compiler_params=_cparams(("parallel",), VMEM_LIMIT),
        name="out_proj",
    )(merged, w_out, x, post_w.reshape(1, d), next_pre_w.reshape(1, d))


def _permute_w_in(w_in):
    depth, d, _ = w_in.shape
    na = w_in[:, :, :NA_COLS]
    diff = w_in[:, :, NA_COLS:NA_COLS + DIFF_COLS]
    g0 = NA_COLS + DIFF_COLS
    gla_qkv = w_in[:, :, g0:g0 + GLA_QKV_COLS]
    gl = w_in[:, :, g0 + GLA_QKV_COLS:g0 + GLA_COLS]
    m0 = g0 + GLA_COLS
    cq_ckv = w_in[:, :, m0:m0 + MLA_Q_LORA + MLA_KV_LORA]
    kr = w_in[:, :, m0 + MLA_Q_LORA + MLA_KV_LORA:m0 + MLA_COLS]
    z = w_in[:, :, m0 + MLA_COLS:]
    used = COL_KRGL + MLA_ROPE + 2 * GLA_RANK
    pad = jnp.zeros((depth, d, PROJ_COLS - used), w_in.dtype)
    return jnp.concatenate([na, diff, gla_qkv, cq_ckv, z, kr, gl, pad], axis=-1).astype(BF16)


def _pad_w_qb(w_qb):
    depth, r, _ = w_qb.shape
    w = w_qb.reshape(depth, r, N_HEADS, MLA_NOPE + MLA_ROPE)
    w = jnp.pad(w, ((0, 0), (0, 0), (0, 0), (0, VAUG_W - MLA_NOPE - MLA_ROPE)))
    return w.reshape(depth, r, N_HEADS * VAUG_W).astype(BF16)


def _rope_tables(s):
    half = DIFF_D // 2
    inv = ROPE_THETA ** (-jnp.arange(0, DIFF_D, 2, dtype=F32) / DIFF_D)
    ang = jnp.arange(s, dtype=F32)[:, None] * inv[None, :]
    cos, sin = jnp.cos(ang), jnp.sin(ang)
    zero = jnp.zeros_like(sin)
    reps = LANES // DIFF_D
    c = jnp.tile(jnp.concatenate([cos, cos], axis=1), (1, reps))
    s1 = jnp.tile(jnp.concatenate([-sin, zero], axis=1), (1, reps))
    s2 = jnp.tile(jnp.concatenate([zero, sin], axis=1), (1, reps))
    assert half * 2 * reps == LANES
    return c, s1, s2


def _trunk(x, p):
    b, s, d = x.shape
    n = b * s
    tables = _rope_tables(s)
    xf = x.reshape(n, d)
    h = _rmsnorm(xf, p["pre_norm_w"][0])
    for l in range(DEPTH):
        lam_init = 0.8 - 0.6 * math.exp(-0.3 * l)
        proj = _in_proj(h, p["w_in"][l])
        proj3 = proj.reshape(b, s, PROJ_COLS)
        dq, dk, dva, mq, mk, mva = _prep(proj3, tables, p["mla_q_norm_w"][l], p["mla_w_qb"][l],
                                         p["mla_kv_norm_w"][l], p["mla_w_kvb"][l])
        o_a = _na(proj3, p["na_bias"][l])
        o_b = _flash(dq, dk, dva, n_maps=2, lam_qk=p["diff_lambda_qk"][l],
                     subw=p["diff_subln_w"][l], lam_init=lam_init)
        o_c = _gla(proj3, p["gla_w_gate_up"][l], p["gla_b_gate"][l], p["gla_norm_w"][l])
        o_d = _flash(mq, mk, mva, n_maps=1)
        branches = [o.reshape(n, BRANCH_W) for o in (o_a, o_b, o_c, o_d)]
        merged = _merge(h, branches, proj, p["w_gate"][l], p["w_branch"][l])
        next_pre = p["pre_norm_w"][(l + 1) % DEPTH]
        xf, h = _out_proj(merged, p["w_out"][l], xf, p["post_norm_w"][l], next_pre)
    return xf.reshape(b, s, d)


def kernel(x_prompt, x_sample, pre_norm_w, w_in, na_rpb, diff_lambda_qk, diff_subln_w, gla_w_gate_up, gla_b_gate, gla_norm_w, mla_q_norm_w, mla_w_qb, mla_kv_norm_w, mla_w_kvb, w_gate, w_branch, w_out, post_norm_w):
    p = {
        "pre_norm_w": pre_norm_w,
        "w_in": _permute_w_in(w_in),
        "na_bias": _na_bias_tiles(na_rpb),
        "diff_lambda_qk": diff_lambda_qk,
        "diff_subln_w": diff_subln_w,
        "gla_w_gate_up": gla_w_gate_up,
        "gla_b_gate": gla_b_gate,
        "gla_norm_w": gla_norm_w,
        "mla_q_norm_w": mla_q_norm_w,
        "mla_w_qb": _pad_w_qb(mla_w_qb),
        "mla_kv_norm_w": mla_kv_norm_w,
        "mla_w_kvb": mla_w_kvb.astype(BF16),
        "w_gate": w_gate.astype(BF16),
        "w_branch": w_branch.astype(BF16),
        "w_out": w_out.astype(BF16),
        "post_norm_w": post_norm_w,
    }
    return (_trunk(x_prompt, p), _trunk(x_sample, p))
```

```python
import functools
import math

import jax
import jax.numpy as jnp
import numpy as np
from jax import lax
from jax.experimental import pallas as pl
from jax.experimental.pallas import tpu as pltpu

F32 = jnp.float32
BF16 = jnp.bfloat16

DEPTH = 4
GRID_W = 64
N_BRANCH = 4
BRANCH_W = 512
N_HEADS = 4
NA_HEAD_DIM = 128
WIN_R = 8
WIN_C = 16
DIFF_D = 64
GLA_DK = 64
GLA_DV = 128
GLA_RANK = 16
GLA_TAU = 16.0
GLA_CHUNK = 64
MLA_Q_LORA = 384
MLA_KV_LORA = 128
MLA_NOPE = 128
MLA_ROPE = 64
MLA_V = 128
ROPE_THETA = 10000.0
EPS = 1e-6
LOG2E = math.log2(math.e)

NA_COLS = 3 * N_HEADS * NA_HEAD_DIM
DIFF_COLS = 3 * N_HEADS * 2 * DIFF_D
GLA_QKV_COLS = 2 * N_HEADS * GLA_DK + N_HEADS * GLA_DV
GLA_COLS = GLA_QKV_COLS + 2 * GLA_RANK
MLA_COLS = MLA_Q_LORA + MLA_KV_LORA + MLA_ROPE
Z_COLS = N_BRANCH * BRANCH_W

LANES = 128
V7X_VMEM_BYTES = 64 * 1024 * 1024
VMEM_LIMIT = 56 * 1024 * 1024
MIB = 1024 * 1024

COL_NA = 0
COL_DIFF = COL_NA + NA_COLS
COL_GLA = COL_DIFF + DIFF_COLS
COL_MLA = COL_GLA + GLA_QKV_COLS
COL_Z = COL_MLA + MLA_Q_LORA + MLA_KV_LORA
COL_KRGL = COL_Z + Z_COLS
PROJ_TN = 1152
PROJ_COLS = 6 * PROJ_TN
GL_LANE0 = MLA_ROPE

NA_QROWS = 8
NA_KROWS = 16
NA_TQ = NA_QROWS * GRID_W
NA_TK = NA_KROWS * GRID_W
NA_KV_PARTS = 4
NEG_BIG = -1e30

VAUG_W = 2 * LANES
FLASH_TQ = 1024
FLASH_TK = 1024
FLASH_UNROLL = 2
GLA_T = 256
GLA_ROWS_PER_STEP = 2


def _cparams(sem, vmem=None):
    return pltpu.CompilerParams(dimension_semantics=sem, vmem_limit_bytes=vmem)


def _nt_dot(a, b):
    return lax.dot_general(a, b, (((1,), (1,)), ((), ())), preferred_element_type=F32)


def _tn_dot(a, b):
    return lax.dot_general(a, b, (((0,), (0,)), ((), ())), preferred_element_type=F32)


def _rmsnorm_kernel(x_ref, w_ref, o_ref):
    x = x_ref[...]
    y = x * lax.rsqrt(jnp.mean(x * x, axis=-1, keepdims=True) + EPS)
    o_ref[...] = (y * w_ref[...]).astype(o_ref.dtype)


def _rmsnorm(x, w):
    n, d = x.shape
    tm = min(512, n)
    return pl.pallas_call(
        _rmsnorm_kernel,
        out_shape=jax.ShapeDtypeStruct((n, d), BF16),
        grid=(n // tm,),
        in_specs=[pl.BlockSpec((tm, d), lambda i: (i, 0)),
                  pl.BlockSpec((1, d), lambda i: (0, 0))],
        out_specs=pl.BlockSpec((tm, d), lambda i: (i, 0)),
        compiler_params=_cparams(("parallel",)),
        name="pre_norm",
    )(x, w.reshape(1, d))


def _matmul_kernel(a_ref, b_ref, o_ref):
    o_ref[...] = jnp.dot(a_ref[...], b_ref[...], preferred_element_type=F32).astype(o_ref.dtype)


def _in_proj(h, w):
    n, d = h.shape
    c = w.shape[1]
    tm = min(1024, n)
    tn = PROJ_TN
    return pl.pallas_call(
        _matmul_kernel,
        out_shape=jax.ShapeDtypeStruct((n, c), BF16),
        grid=(n // tm, c // tn),
        in_specs=[pl.BlockSpec((tm, d), lambda i, j: (i, 0)),
                  pl.BlockSpec((d, tn), lambda i, j: (0, j))],
        out_specs=pl.BlockSpec((tm, tn), lambda i, j: (i, j)),
        compiler_params=_cparams(("parallel", "parallel"), VMEM_LIMIT),
        name="in_proj",
    )(h, w)


def _rope_lanes(x, c, s1, s2):
    return x * c + pltpu.roll(x, LANES - 32, 1) * s1 + pltpu.roll(x, 32, 1) * s2


def _prep_kernel(dq_ref, dk_ref, dv_ref, cqkv_ref, krgl_ref, cos_ref, s1_ref, s2_ref,
                 qnw_ref, wqb_ref, kvnw_ref, wkvb_ref,
                 dqo_ref, dko_ref, dvo_ref, mqo_ref, mko_ref, mvo_ref):
    c, s1, s2 = cos_ref[...], s1_ref[...], s2_ref[...]
    t = c.shape[0]
    lane = lax.broadcasted_iota(jnp.int32, (t, LANES), 1)
    ones_col = jnp.where(lane == 0, 1.0, 0.0).astype(BF16)
    diff_qscale = (DIFF_D ** -0.5) * LOG2E
    mla_qscale = ((MLA_NOPE + MLA_ROPE) ** -0.5) * LOG2E

    for hd in range(N_HEADS):
        sl = slice(hd * LANES, (hd + 1) * LANES)
        q = _rope_lanes(dq_ref[0, :, sl].astype(F32), c, s1, s2) * diff_qscale
        dqo_ref[0, :, sl] = q.astype(BF16)
        k = _rope_lanes(dk_ref[0, :, sl].astype(F32), c, s1, s2)
        dko_ref[0, :, sl] = k.astype(BF16)
        dvo_ref[0, :, hd * VAUG_W:hd * VAUG_W + LANES] = dv_ref[0, :, sl]
        dvo_ref[0, :, hd * VAUG_W + LANES:(hd + 1) * VAUG_W] = ones_col

    cq = cqkv_ref[0, :, :MLA_Q_LORA].astype(F32)
    cqn = cq * lax.rsqrt(jnp.mean(cq * cq, axis=-1, keepdims=True) + EPS) * qnw_ref[...]
    qm = jnp.dot(cqn.astype(BF16), wqb_ref[...], preferred_element_type=F32)
    ckv = cqkv_ref[0, :, MLA_Q_LORA:].astype(F32)
    ckvn = ckv * lax.rsqrt(jnp.mean(ckv * ckv, axis=-1, keepdims=True) + EPS) * kvnw_ref[...]
    kv = jnp.dot(ckvn.astype(BF16), wkvb_ref[...], preferred_element_type=F32)
    kr = jnp.where(lane < MLA_ROPE, krgl_ref[0].astype(F32), 0.0)
    kr = _rope_lanes(kr, c, s1, s2).astype(BF16)
    for hd in range(N_HEADS):
        lo, mid, hi = hd * VAUG_W, hd * VAUG_W + LANES, (hd + 1) * VAUG_W
        mqo_ref[0, :, lo:mid] = (qm[:, lo:mid] * mla_qscale).astype(BF16)
        qr = _rope_lanes(qm[:, mid:hi], c, s1, s2) * mla_qscale
        mqo_ref[0, :, mid:hi] = qr.astype(BF16)
        mko_ref[0, :, lo:mid] = kv[:, lo:mid].astype(BF16)
        mko_ref[0, :, mid:hi] = kr
        mvo_ref[0, :, lo:mid] = kv[:, mid:hi].astype(BF16)
        mvo_ref[0, :, mid:hi] = ones_col


def _prep(proj3, tables, qnw, wqb, kvnw, wkvb):
    b, s, _ = proj3.shape
    tm = min(512, s)
    cos, s1, s2 = tables
    hw = N_HEADS * LANES
    aw = N_HEADS * VAUG_W
    tok = lambda cb: (lambda bi, i: (bi, i, cb))
    tab = pl.BlockSpec((tm, LANES), lambda bi, i: (i, 0))
    const = lambda shape: pl.BlockSpec(shape, lambda bi, i: (0, 0))
    out_sd = lambda w: jax.ShapeDtypeStruct((b, s, w), BF16)
    out_bs = lambda w: pl.BlockSpec((1, tm, w), lambda bi, i: (bi, i, 0))
    return pl.pallas_call(
        _prep_kernel,
        out_shape=(out_sd(hw), out_sd(hw), out_sd(aw), out_sd(aw), out_sd(aw), out_sd(aw)),
        grid=(b, s // tm),
        in_specs=[pl.BlockSpec((1, tm, hw), tok(COL_DIFF // hw)),
                  pl.BlockSpec((1, tm, hw), tok(COL_DIFF // hw + 1)),
                  pl.BlockSpec((1, tm, hw), tok(COL_DIFF // hw + 2)),
                  pl.BlockSpec((1, tm, hw), tok(COL_MLA // hw)),
                  pl.BlockSpec((1, tm, LANES), tok(COL_KRGL // LANES)),
                  tab, tab, tab,
                  const((1, MLA_Q_LORA)), const(wqb.shape),
                  const((1, MLA_KV_LORA)), const(wkvb.shape)],
        out_specs=(out_bs(hw), out_bs(hw), out_bs(aw), out_bs(aw), out_bs(aw), out_bs(aw)),
        compiler_params=_cparams(("parallel", "parallel"), VMEM_LIMIT),
        name="branch_prep",
    )(proj3, proj3, proj3, proj3, proj3, cos, s1, s2,
      qnw.reshape(1, -1), wqb, kvnw.reshape(1, -1), wkvb)


def _flash_kernel(*refs, n_maps, tk, lam_init):
    if n_maps == 2:
        q_ref, kt_ref, v_ref, lam_ref, subw_ref, o_ref, m_scr, acc_scr, s_scr = refs
    else:
        q_ref, kt_ref, v_ref, o_ref, m_scr, acc_scr, s_scr = refs
    nk = kt_ref.shape[2] // tk
    assert nk >= 2 and nk % 2 == 0
    q = q_ref[0]
    if n_maps == 2:
        lane = lax.broadcasted_iota(jnp.int32, q.shape, 1)
        zero = jnp.zeros_like(q)
        qs = (jnp.where(lane < DIFF_D, q, zero), jnp.where(lane >= DIFF_D, q, zero))
    else:
        qs = (q,)

    m_scr[...] = jnp.full(m_scr.shape, -jnp.inf, F32)
    acc_scr[...] = jnp.zeros(acc_scr.shape, F32)

    def scores(j, slot):
        kt = kt_ref[0, :, pl.ds(pl.multiple_of(j * tk, tk), tk)]
        for mi in range(n_maps):
            s_scr[slot, mi] = jnp.dot(qs[mi], kt, preferred_element_type=F32)

    def accumulate(j, slot):
        v = v_ref[0, pl.ds(pl.multiple_of(j * tk, tk), tk), :]
        for mi in range(n_maps):
            s = s_scr[slot, mi]
            m_prev = m_scr[mi]
            m_new = jnp.maximum(m_prev, jnp.max(s, axis=1, keepdims=True))
            alpha = jnp.exp2(m_prev - m_new)
            p = jnp.exp2(s - jnp.tile(m_new, (1, tk // LANES)))
            pv = jnp.dot(p.astype(BF16), v, preferred_element_type=F32)
            acc_scr[mi] = acc_scr[mi] * jnp.tile(alpha, (1, VAUG_W // LANES)) + pv
            m_scr[mi] = m_new

    unroll = min(FLASH_UNROLL, nk)
    assert nk % unroll == 0 and unroll % 2 == 0
    scores(0, 0)

    def body(i, carry):
        j0 = unroll * i
        for t in range(unroll):
            scores(j0 + t + 1, (t + 1) % 2)
            accumulate(j0 + t, t % 2)
        return carry

    lax.fori_loop(0, nk // unroll - 1, body, 0)
    j0 = nk - unroll
    for t in range(unroll):
        if t + 1 < unroll:
            scores(j0 + t + 1, (t + 1) % 2)
        accumulate(j0 + t, t % 2)

    outs = []
    for mi in range(n_maps):
        acc = acc_scr[mi]
        outs.append(acc[:, :LANES] / acc[:, LANES:LANES + 1])
    if n_maps == 2:
        lq = lam_ref[...]
        lam = (jnp.exp(jnp.sum(lq[0:1] * lq[1:2], axis=1, keepdims=True))
               - jnp.exp(jnp.sum(lq[2:3] * lq[3:4], axis=1, keepdims=True)) + lam_init)
        o = outs[0] - lam * outs[1]
        y = o * lax.rsqrt(jnp.mean(o * o, axis=-1, keepdims=True) + EPS) * subw_ref[...]
        o = y * (1.0 - lam_init)
    else:
        o = outs[0]
    o_ref[0] = o.astype(o_ref.dtype)


def _flash(q, k, vaug, *, n_maps, lam_qk=None, subw=None, lam_init=0.0):
    b, s, qw_all = q.shape
    qw = qw_all // N_HEADS
    kw = k.shape[2] // N_HEADS
    kt = jnp.swapaxes(k, 1, 2)
    tq = min(FLASH_TQ, s)
    tk = min(FLASH_TK, s // 2)
    in_specs = [pl.BlockSpec((1, tq, qw), lambda bi, hd, i: (bi, i, hd)),
                pl.BlockSpec((1, kw, s), lambda bi, hd, i: (bi, hd, 0)),
                pl.BlockSpec((1, s, VAUG_W), lambda bi, hd, i: (bi, 0, hd))]
    args = [q, kt, vaug]
    if n_maps == 2:
        in_specs += [pl.BlockSpec(lam_qk.shape, lambda bi, hd, i: (0, 0)),
                     pl.BlockSpec((1, LANES), lambda bi, hd, i: (0, 0))]
        args += [lam_qk, subw.reshape(1, LANES)]
    return pl.pallas_call(
        functools.partial(_flash_kernel, n_maps=n_maps, tk=tk, lam_init=lam_init),
        out_shape=jax.ShapeDtypeStruct((b, s, N_HEADS * LANES), BF16),
        grid=(b, N_HEADS, s // tq),
        in_specs=in_specs,
        out_specs=pl.BlockSpec((1, tq, LANES), lambda bi, hd, i: (bi, i, hd)),
        scratch_shapes=[pltpu.VMEM((n_maps, tq, LANES), F32),
                        pltpu.VMEM((n_maps, tq, VAUG_W), F32),
                        pltpu.VMEM((2, n_maps, tq, tk), F32)],
        compiler_params=_cparams(("parallel", "parallel", "parallel"), VMEM_LIMIT),
        name="diff_attention" if n_maps == 2 else "latent_attention",
    )(*args)


def _na_kernel(q_ref, *refs):
    k_refs, v_refs = refs[:NA_KV_PARTS], refs[NA_KV_PARTS:2 * NA_KV_PARTS]
    b_ref, o_ref = refs[2 * NA_KV_PARTS:]
    for hd in range(N_HEADS):
        sl = slice(hd * LANES, (hd + 1) * LANES)
        k = jnp.concatenate([r[0, :, sl] for r in k_refs], axis=0)
        v = jnp.concatenate([r[0, :, sl] for r in v_refs], axis=0)
        s = _nt_dot(q_ref[0, :, sl], k) * (NA_HEAD_DIM ** -0.5) + b_ref[0, hd]
        m = jnp.max(s, axis=1, keepdims=True)
        p = jnp.exp(s - m)
        l = jnp.sum(p, axis=1, keepdims=True)
        o = jnp.dot(p.astype(BF16), v, preferred_element_type=F32) / l
        o_ref[0, :, sl] = o.astype(o_ref.dtype)


def _na_bias_tiles(rpb):
    a = np.arange(NA_QROWS)
    w = np.arange(NA_KROWS)
    c = np.arange(GRID_W)
    cs = np.clip(c - WIN_C // 2, 0, GRID_W - WIN_C)
    col_ok = (c[None, :] >= cs[:, None]) & (c[None, :] < cs[:, None] + WIN_C)
    dc = c[None, :] - c[:, None] + (WIN_C - 1)
    sel_c = ((dc[..., None] == np.arange(2 * WIN_C - 1)) & col_ok[..., None]).astype(np.float32)
    half = WIN_R // 2
    sel_r, row_oks = [], []
    for off, wstart in ((0, np.maximum(a - half, 0)),
                        (half, a),
                        (NA_QROWS, np.minimum(a + half, NA_QROWS))):
        row_ok = (w[None, :] >= wstart[:, None]) & (w[None, :] < wstart[:, None] + WIN_R)
        dr = w[None, :] - a[:, None] - off + (WIN_R - 1)
        sel_r.append(((dr[..., None] == np.arange(2 * WIN_R - 1)) & row_ok[..., None]).astype(np.float32))
        row_oks.append(row_ok)
    sel_r = np.stack(sel_r)
    ok = np.stack(row_oks)[:, :, None, :, None] & col_ok[None, None, :, None, :]
    toep = jnp.einsum("qkc,lhdc->lhdqk", sel_c, rpb.astype(F32), precision=lax.Precision.HIGHEST)
    tiles = jnp.einsum("tawd,lhdqk->lthaqwk", sel_r, toep, precision=lax.Precision.HIGHEST)
    tiles = jnp.where(ok[None, :, None], tiles, NEG_BIG)
    return tiles.reshape(rpb.shape[0], 3, N_HEADS, NA_TQ, NA_TK)


def _na(proj3, bias_tiles):
    b, s, _ = proj3.shape
    rows = s // GRID_W
    assert rows % NA_QROWS == 0 and rows >= NA_KROWS
    nq = rows // NA_QROWS
    hw = N_HEADS * LANES
    part_rows = NA_KROWS // NA_KV_PARTS
    part = part_rows * GRID_W
    qb = COL_NA // hw

    def kv_spec(cb, t):
        def idx(bi, i):
            ks_row = jnp.clip(i * NA_QROWS - WIN_R // 2, 0, rows - NA_KROWS)
            return (bi, ks_row // part_rows + t, cb)
        return pl.BlockSpec((1, part, hw), idx)

    def bias_idx(bi, i):
        kind = jnp.where(i == 0, 0, jnp.where(i == nq - 1, 2, 1))
        return (kind, 0, 0, 0)

    return pl.pallas_call(
        _na_kernel,
        out_shape=jax.ShapeDtypeStruct((b, s, hw), BF16),
        grid=(b, nq),
        in_specs=([pl.BlockSpec((1, NA_TQ, hw), lambda bi, i: (bi, i, qb))]
                  + [kv_spec(qb + 1, t) for t in range(NA_KV_PARTS)]
                  + [kv_spec(qb + 2, t) for t in range(NA_KV_PARTS)]
                  + [pl.BlockSpec((1, N_HEADS, NA_TQ, NA_TK), bias_idx)]),
        out_specs=pl.BlockSpec((1, NA_TQ, hw), lambda bi, i: (bi, i, 0)),
        compiler_params=_cparams(("parallel", "parallel"), VMEM_LIMIT),
        name="neighborhood_attention",
    )(proj3, *([proj3] * (2 * NA_KV_PARTS)), bias_tiles)


def _gla_kernel(*refs, reverse):
    if reverse:
        q_ref, k_ref, v_ref, gl_ref, wg_ref, bg_ref, ofwd_ref, nw_ref, o_ref, st_scr = refs
    else:
        q_ref, k_ref, v_ref, gl_ref, wg_ref, bg_ref, o_ref, st_scr = refs
    rows_per_step, t = q_ref.shape[0], q_ref.shape[1]
    gw = N_HEADS * GLA_DK
    c_len = GLA_CHUNK

    @pl.when(pl.program_id(1) == 0)
    def _():
        st_scr[...] = jnp.zeros(st_scr.shape, F32)

    pos = lax.broadcasted_iota(jnp.int32, (t, gw), 0) & (c_len - 1)
    ref_i = c_len // 2 if reverse else c_len // 2 - 1
    last_i = 0 if reverse else c_len - 1
    n_chunks = t // c_len
    order = range(n_chunks - 1, -1, -1) if reverse else range(n_chunks)
    ri = lax.broadcasted_iota(jnp.int32, (t, t), 0)
    ci = lax.broadcasted_iota(jnp.int32, (t, t), 1)
    same_chunk = (ri // c_len) == (ci // c_len)
    intra = same_chunk & ((ci >= ri) if reverse else (ri >= ci))
    lane = lax.broadcasted_iota(jnp.int32, (t, LANES), 1)
    row_chunk = lax.broadcasted_iota(jnp.int32, (t, LANES), 0) // c_len

    def spread(x):
        return jnp.concatenate([jnp.where(row_chunk == n, x, 0.0) for n in range(n_chunks)], axis=1)

    def per_chunk_rows(x, i):
        return jnp.concatenate([jnp.broadcast_to(x[n * c_len + i:n * c_len + i + 1], (c_len, x.shape[1]))
                                for n in range(n_chunks)], axis=0)

    for bi in range(rows_per_step):
        gpre = jnp.dot(gl_ref[bi], wg_ref[...], preferred_element_type=F32) + bg_ref[...]
        g = (jnp.minimum(gpre, 0.0) - jnp.log1p(jnp.exp(-jnp.abs(gpre)))) * (1.0 / GLA_TAU)

        cum = g
        for sh in (1, 2, 4, 8, 16, 32):
            if reverse:
                cum = cum + jnp.where(pos < c_len - sh, pltpu.roll(cum, t - sh, 0), 0.0)
            else:
                cum = cum + jnp.where(pos >= sh, pltpu.roll(cum, sh, 0), 0.0)

        ref = per_chunk_rows(cum, ref_i)
        last = per_chunk_rows(cum, last_i)
        qf = q_ref[bi].astype(F32) * (GLA_DK ** -0.5)
        kf = k_ref[bi].astype(F32)
        qd = qf * jnp.exp(cum - ref)
        kd = (kf * jnp.exp(ref - cum)).astype(BF16)
        qe = qf * jnp.exp(cum)
        kl = kf * jnp.exp(last - cum)
        dec = jnp.exp(last)

        for pr in range(N_HEADS // 2):
            lsl = slice(pr * LANES, (pr + 1) * LANES)
            owns = (lane < GLA_DK, lane >= GLA_DK)
            qd_p, qe_p = qd[:, lsl], qe[:, lsl]
            lhs = jnp.concatenate([jnp.where(own, qd_p, 0.0) for own in owns], axis=0).astype(BF16)
            a_pair = _nt_dot(lhs, kd[:, lsl])
            kl_spread = spread(kl[:, lsl]).astype(BF16)
            for hh in range(2):
                hd = 2 * pr + hh
                vsl = slice(hd * GLA_DV, (hd + 1) * GLA_DV)
                v_h = v_ref[bi, :, vsl]
                a = jnp.where(intra, a_pair[hh * t:(hh + 1) * t], 0.0)
                o_h = jnp.dot(a.astype(BF16), v_h, preferred_element_type=F32)
                ds_all = _tn_dot(v_h, kl_spread)
                st = st_scr[bi, hd]
                before = [None] * n_chunks
                for n in order:
                    before[n] = st
                    dec_n = dec[n * c_len:n * c_len + 1, lsl]
                    st = st * dec_n + ds_all[:, n * LANES:(n + 1) * LANES]
                st_scr[bi, hd] = st
                st_cat = jnp.concatenate(before, axis=1).astype(BF16)
                qe_spread = spread(jnp.where(owns[hh], qe_p, 0.0)).astype(BF16)
                o_h = o_h + _nt_dot(qe_spread, st_cat)
                if reverse:
                    tot = o_h + ofwd_ref[bi, :, vsl]
                    y = tot * lax.rsqrt(jnp.mean(tot * tot, axis=-1, keepdims=True) + EPS) * nw_ref[...]
                    o_ref[bi, :, vsl] = y.astype(o_ref.dtype)
                else:
                    o_ref[bi, :, vsl] = o_h


def _gla_pass(proj3, wg, bg, reverse, o_fwd=None, norm_w=None):
    b, s, _ = proj3.shape
    t = min(GLA_T, s)
    nb = s // t
    gw = N_HEADS * GLA_DK
    vw = N_HEADS * GLA_DV
    blk = (lambda i: nb - 1 - i) if reverse else (lambda i: i)
    tok = lambda cb: (lambda bi, i: (bi, blk(i), cb))
    const = lambda shape: pl.BlockSpec(shape, lambda bi, i: (0, 0))
    rb = GLA_ROWS_PER_STEP
    assert b % rb == 0
    in_specs = [pl.BlockSpec((rb, t, gw), tok(COL_GLA // gw)),
                pl.BlockSpec((rb, t, gw), tok(COL_GLA // gw + 1)),
                pl.BlockSpec((rb, t, vw), tok((COL_GLA + 2 * gw) // vw)),
                pl.BlockSpec((rb, t, LANES), tok(COL_KRGL // LANES)),
                const((LANES, gw)), const((1, gw))]
    args = [proj3, proj3, proj3, proj3, wg, bg]
    if reverse:
        in_specs += [pl.BlockSpec((rb, t, vw), tok(0)), const((1, GLA_DV))]
        args += [o_fwd, norm_w.reshape(1, GLA_DV)]
    return pl.pallas_call(
        functools.partial(_gla_kernel, reverse=reverse),
        out_shape=jax.ShapeDtypeStruct((b, s, vw), BF16 if reverse else F32),
        grid=(b // rb, nb),
        in_specs=in_specs,
        out_specs=pl.BlockSpec((rb, t, vw), tok(0)),
        scratch_shapes=[pltpu.VMEM((rb, N_HEADS, GLA_DV, LANES), F32)],
        compiler_params=_cparams(("parallel", "arbitrary")),
        name="gla_backward" if reverse else "gla_forward",
    )(*args)


def _gla(proj3, w_gate_up, b_gate, norm_w):
    gw = N_HEADS * GLA_DK
    wgs = []
    for d in range(2):
        lo = GL_LANE0 + d * GLA_RANK
        wgs.append(jnp.zeros((LANES, gw), F32).at[lo:lo + GLA_RANK].set(w_gate_up[d]).astype(BF16))
    o_fwd = _gla_pass(proj3, wgs[0], b_gate[0].reshape(1, gw), False)
    return _gla_pass(proj3, wgs[1], b_gate[1].reshape(1, gw), True, o_fwd, norm_w)


def _merge_kernel(h_ref, oa_ref, ob_ref, oc_ref, od_ref, za_ref, zb_ref, zc_ref, zd_ref,
                  wg_ref, wb_ref, o_ref):
    h = h_ref[...]
    acc = None
    for i, (o_r, z_r) in enumerate(((oa_ref, za_ref), (ob_ref, zb_ref), (oc_ref, zc_ref), (od_ref, zd_ref))):
        z = z_r[...].astype(F32)
        u = (o_r[...].astype(F32) * (z * jax.nn.sigmoid(z))).astype(BF16)
        t = jnp.dot(u, wb_ref[i], preferred_element_type=F32)
        gate = jax.nn.sigmoid(jnp.dot(h, wg_ref[i], preferred_element_type=F32))
        acc = gate * t if acc is None else acc + gate * t
    o_ref[...] = acc.astype(o_ref.dtype)


def _merge(h, branches, proj, wg, wb):
    n, d = h.shape
    tm = min(512, n)
    tn = min(512, d)
    zb0 = COL_Z // BRANCH_W
    tok = pl.BlockSpec((tm, BRANCH_W), lambda j, i: (i, 0))
    zspec = lambda k: pl.BlockSpec((tm, BRANCH_W), lambda j, i: (i, zb0 + k))
    return pl.pallas_call(
        _merge_kernel,
        out_shape=jax.ShapeDtypeStruct((n, d), BF16),
        grid=(d // tn, n // tm),
        in_specs=[pl.BlockSpec((tm, d), lambda j, i: (i, 0)), tok, tok, tok, tok,
                  zspec(0), zspec(1), zspec(2), zspec(3),
                  pl.BlockSpec((N_BRANCH, d, tn), lambda j, i: (0, 0, j)),
                  pl.BlockSpec((N_BRANCH, BRANCH_W, tn), lambda j, i: (0, 0, j))],
        out_specs=pl.BlockSpec((tm, tn), lambda j, i: (i, j)),
        compiler_params=_cparams(("parallel", "parallel"), VMEM_LIMIT),
        name="branch_merge",
    )(h, *branches, proj, proj, proj, proj, wg, wb)


def _out_kernel(m_ref, w_ref, x_ref, pw_ref, *rest):
    y = jnp.dot(m_ref[...], w_ref[...], preferred_element_type=F32)
    y = y * lax.rsqrt(jnp.mean(y * y, axis=-1, keepdims=True) + EPS) * pw_ref[...]
    xn = x_ref[...] + y
    if len(rest) == 1:
        (xo_ref,) = rest
    else:
        nw_ref, xo_ref, ho_ref = rest
        hn = xn * lax.rsqrt(jnp.mean(xn * xn, axis=-1, keepdims=True) + EPS) * nw_ref[...]
        ho_ref[...] = hn.astype(ho_ref.dtype)
    xo_ref[...] = xn


def _out_proj(merged, w_out, x, post_w, next_pre_w=None):
    n, d = x.shape
    tm = min(512, n)
    tok = lambda: pl.BlockSpec((tm, d), lambda i: (i, 0))
    vec = lambda: pl.BlockSpec((1, d), lambda i: (0, 0))
    in_specs = [tok(), pl.BlockSpec((d, d), lambda i: (0, 0)), tok(), vec()]
    args = [merged, w_out, x, post_w.reshape(1, d)]
    out_shape = [jax.ShapeDtypeStruct((n, d), F32)]
    out_specs = [tok()]
    if next_pre_w is not None:
        in_specs.append(vec())
        args.append(next_pre_w.reshape(1, d))
        out_shape.append(jax.ShapeDtypeStruct((n, d), BF16))
        out_specs.append(tok())
    return pl.pallas_call(
        _out_kernel,
        out_shape=tuple(out_shape),
        grid=(n // tm,),
        in_specs=in_specs,
        out_specs=tuple(out_specs),
        compiler_params=_cparams(("parallel",), VMEM_LIMIT),
        name="out_proj",
    )(*args)


def _permute_w_in(w_in):
    depth, d, _ = w_in.shape
    na = w_in[:, :, :NA_COLS]
    diff = w_in[:, :, NA_COLS:NA_COLS + DIFF_COLS]
    g0 = NA_COLS + DIFF_COLS
    gla_qkv = w_in[:, :, g0:g0 + GLA_QKV_COLS]
    gl = w_in[:, :, g0 + GLA_QKV_COLS:g0 + GLA_COLS]
    m0 = g0 + GLA_COLS
    cq_ckv = w_in[:, :, m0:m0 + MLA_Q_LORA + MLA_KV_LORA]
    kr = w_in[:, :, m0 + MLA_Q_LORA + MLA_KV_LORA:m0 + MLA_COLS]
    z = w_in[:, :, m0 + MLA_COLS:]
    used = COL_KRGL + MLA_ROPE + 2 * GLA_RANK
    pad = jnp.zeros((depth, d, PROJ_COLS - used), w_in.dtype)
    return jnp.concatenate([na, diff, gla_qkv, cq_ckv, z, kr, gl, pad], axis=-1).astype(BF16)


def _pad_w_qb(w_qb):
    depth, r, _ = w_qb.shape
    w = w_qb.reshape(depth, r, N_HEADS, MLA_NOPE + MLA_ROPE)
    w = jnp.pad(w, ((0, 0), (0, 0), (0, 0), (0, VAUG_W - MLA_NOPE - MLA_ROPE)))
    return w.reshape(depth, r, N_HEADS * VAUG_W).astype(BF16)


def _rope_tables(s):
    half = DIFF_D // 2
    inv = ROPE_THETA ** (-jnp.arange(0, DIFF_D, 2, dtype=F32) / DIFF_D)
    ang = jnp.arange(s, dtype=F32)[:, None] * inv[None, :]
    cos, sin = jnp.cos(ang), jnp.sin(ang)
    zero = jnp.zeros_like(sin)
    reps = LANES // DIFF_D
    c = jnp.tile(jnp.concatenate([cos, cos], axis=1), (1, reps))
    s1 = jnp.tile(jnp.concatenate([-sin, zero], axis=1), (1, reps))
    s2 = jnp.tile(jnp.concatenate([zero, sin], axis=1), (1, reps))
    assert half * 2 * reps == LANES
    return c, s1, s2


def _trunk(x, p):
    b, s, d = x.shape
    n = b * s
    tables = _rope_tables(s)
    xf = x.reshape(n, d)
    h = _rmsnorm(xf, p["pre_norm_w"][0])
    for l in range(DEPTH):
        lam_init = 0.8 - 0.6 * math.exp(-0.3 * l)
        proj = _in_proj(h, p["w_in"][l])
        proj3 = proj.reshape(b, s, PROJ_COLS)
        dq, dk, dva, mq, mk, mva = _prep(proj3, tables, p["mla_q_norm_w"][l], p["mla_w_qb"][l],
                                         p["mla_kv_norm_w"][l], p["mla_w_kvb"][l])
        o_a = _na(proj3, p["na_bias"][l])
        o_b = _flash(dq, dk, dva, n_maps=2, lam_qk=p["diff_lambda_qk"][l],
                     subw=p["diff_subln_w"][l], lam_init=lam_init)
        o_c = _gla(proj3, p["gla_w_gate_up"][l], p["gla_b_gate"][l], p["gla_norm_w"][l])
        o_d = _flash(mq, mk, mva, n_maps=1)
        branches = [o.reshape(n, BRANCH_W) for o in (o_a, o_b, o_c, o_d)]
        merged = _merge(h, branches, proj, p["w_gate"][l], p["w_branch"][l])
        if l + 1 < DEPTH:
            xf, h = _out_proj(merged, p["w_out"][l], xf, p["post_norm_w"][l], p["pre_norm_w"][l + 1])
        else:
            (xf,) = _out_proj(merged, p["w_out"][l], xf, p["post_norm_w"][l])
    return xf.reshape(b, s, d)


def kernel(x_prompt, x_sample, pre_norm_w, w_in, na_rpb, diff_lambda_qk, diff_subln_w, gla_w_gate_up, gla_b_gate, gla_norm_w, mla_q_norm_w, mla_w_qb, mla_kv_norm_w, mla_w_kvb, w_gate, w_branch, w_out, post_norm_w):
    p = {
        "pre_norm_w": pre_norm_w,
        "w_in": _permute_w_in(w_in),
        "na_bias": _na_bias_tiles(na_rpb),
        "diff_lambda_qk": diff_lambda_qk,
        "diff_subln_w": diff_subln_w,
        "gla_w_gate_up": gla_w_gate_up,
        "gla_b_gate": gla_b_gate,
        "gla_norm_w": gla_norm_w,
        "mla_q_norm_w": mla_q_norm_w,
        "mla_w_qb": _pad_w_qb(mla_w_qb),
        "mla_kv_norm_w": mla_kv_norm_w,
        "mla_w_kvb": mla_w_kvb.astype(BF16),
        "w_gate": w_gate.astype(BF16),
        "w_branch": w_branch.astype(BF16),
        "w_out": w_out.astype(BF16),
        "post_norm_w": post_norm_w,
    }
    return (_trunk(x_prompt, p), _trunk(x_sample, p))
```

```python
import functools
import math

import jax
import jax.numpy as jnp
import numpy as np
from jax import lax
from jax.experimental import pallas as pl
from jax.experimental.pallas import tpu as pltpu

F32 = jnp.float32
BF16 = jnp.bfloat16

DEPTH = 4
GRID_W = 64
N_BRANCH = 4
BRANCH_W = 512
N_HEADS = 4
NA_HEAD_DIM = 128
WIN_R = 8
WIN_C = 16
DIFF_D = 64
GLA_DK = 64
GLA_DV = 128
GLA_RANK = 16
GLA_TAU = 16.0
GLA_CHUNK = 64
MLA_Q_LORA = 384
MLA_KV_LORA = 128
MLA_NOPE = 128
MLA_ROPE = 64
MLA_V = 128
ROPE_THETA = 10000.0
EPS = 1e-6
LOG2E = math.log2(math.e)

NA_COLS = 3 * N_HEADS * NA_HEAD_DIM
DIFF_COLS = 3 * N_HEADS * 2 * DIFF_D
GLA_QKV_COLS = 2 * N_HEADS * GLA_DK + N_HEADS * GLA_DV
GLA_COLS = GLA_QKV_COLS + 2 * GLA_RANK
MLA_COLS = MLA_Q_LORA + MLA_KV_LORA + MLA_ROPE
Z_COLS = N_BRANCH * BRANCH_W

LANES = 128
V7X_VMEM_BYTES = 64 * 1024 * 1024
VMEM_LIMIT = 56 * 1024 * 1024
MIB = 1024 * 1024

COL_NA = 0
COL_DIFF = COL_NA + NA_COLS
COL_GLA = COL_DIFF + DIFF_COLS
COL_MLA = COL_GLA + GLA_QKV_COLS
COL_Z = COL_MLA + MLA_Q_LORA + MLA_KV_LORA
COL_KRGL = COL_Z + Z_COLS
PROJ_TN = 1152
PROJ_COLS = 6 * PROJ_TN
GL_LANE0 = MLA_ROPE

NA_QROWS = 8
NA_KROWS = 16
NA_TQ = NA_QROWS * GRID_W
NA_TK = NA_KROWS * GRID_W
NA_KV_PARTS = 4
NEG_BIG = -1e30

VAUG_W = 2 * LANES
FLASH_TQ = 1024
FLASH_TK = 1024
FLASH_UNROLL = 2
MERGE_TN = 512
GLA_T = 256
GLA_ROWS_PER_STEP = 2


def _cparams(sem, vmem=None):
    return pltpu.CompilerParams(dimension_semantics=sem, vmem_limit_bytes=vmem)


def _nt_dot(a, b):
    return lax.dot_general(a, b, (((1,), (1,)), ((), ())), preferred_element_type=F32)


def _tn_dot(a, b):
    return lax.dot_general(a, b, (((0,), (0,)), ((), ())), preferred_element_type=F32)


def _rmsnorm_kernel(x_ref, w_ref, o_ref):
    x = x_ref[...]
    y = x * lax.rsqrt(jnp.mean(x * x, axis=-1, keepdims=True) + EPS)
    o_ref[...] = (y * w_ref[...]).astype(o_ref.dtype)


def _rmsnorm(x, w):
    n, d = x.shape
    tm = min(512, n)
    return pl.pallas_call(
        _rmsnorm_kernel,
        out_shape=jax.ShapeDtypeStruct((n, d), BF16),
        grid=(n // tm,),
        in_specs=[pl.BlockSpec((tm, d), lambda i: (i, 0)),
                  pl.BlockSpec((1, d), lambda i: (0, 0))],
        out_specs=pl.BlockSpec((tm, d), lambda i: (i, 0)),
        compiler_params=_cparams(("parallel",)),
        name="pre_norm",
    )(x, w.reshape(1, d))


def _matmul_kernel(a_ref, b_ref, o_ref):
    o_ref[...] = jnp.dot(a_ref[...], b_ref[...], preferred_element_type=F32).astype(o_ref.dtype)


def _in_proj(h, w):
    n, d = h.shape
    c = w.shape[1]
    tm = min(1024, n)
    tn = PROJ_TN
    return pl.pallas_call(
        _matmul_kernel,
        out_shape=jax.ShapeDtypeStruct((n, c), BF16),
        grid=(n // tm, c // tn),
        in_specs=[pl.BlockSpec((tm, d), lambda i, j: (i, 0)),
                  pl.BlockSpec((d, tn), lambda i, j: (0, j))],
        out_specs=pl.BlockSpec((tm, tn), lambda i, j: (i, j)),
        compiler_params=_cparams(("parallel", "parallel"), VMEM_LIMIT),
        name="in_proj",
    )(h, w)


def _rope_lanes(x, c, s1, s2):
    return x * c + pltpu.roll(x, LANES - 32, 1) * s1 + pltpu.roll(x, 32, 1) * s2


def _prep_kernel(dq_ref, dk_ref, dv_ref, cqkv_ref, krgl_ref, cos_ref, s1_ref, s2_ref,
                 qnw_ref, wqb_ref, kvnw_ref, wkvb_ref,
                 dqo_ref, dko_ref, dvo_ref, mqo_ref, mko_ref, mvo_ref):
    c, s1, s2 = cos_ref[...], s1_ref[...], s2_ref[...]
    t = c.shape[0]
    lane = lax.broadcasted_iota(jnp.int32, (t, LANES), 1)
    ones_col = jnp.where(lane == 0, 1.0, 0.0).astype(BF16)
    diff_qscale = (DIFF_D ** -0.5) * LOG2E
    mla_qscale = ((MLA_NOPE + MLA_ROPE) ** -0.5) * LOG2E

    for hd in range(N_HEADS):
        sl = slice(hd * LANES, (hd + 1) * LANES)
        q = _rope_lanes(dq_ref[0, :, sl].astype(F32), c, s1, s2) * diff_qscale
        dqo_ref[0, :, sl] = q.astype(BF16)
        k = _rope_lanes(dk_ref[0, :, sl].astype(F32), c, s1, s2)
        dko_ref[0, :, sl] = k.astype(BF16)
        dvo_ref[0, :, hd * VAUG_W:hd * VAUG_W + LANES] = dv_ref[0, :, sl]
        dvo_ref[0, :, hd * VAUG_W + LANES:(hd + 1) * VAUG_W] = ones_col

    cq = cqkv_ref[0, :, :MLA_Q_LORA].astype(F32)
    cqn = cq * lax.rsqrt(jnp.mean(cq * cq, axis=-1, keepdims=True) + EPS) * qnw_ref[...]
    qm = jnp.dot(cqn.astype(BF16), wqb_ref[...], preferred_element_type=F32)
    ckv = cqkv_ref[0, :, MLA_Q_LORA:].astype(F32)
    ckvn = ckv * lax.rsqrt(jnp.mean(ckv * ckv, axis=-1, keepdims=True) + EPS) * kvnw_ref[...]
    kv = jnp.dot(ckvn.astype(BF16), wkvb_ref[...], preferred_element_type=F32)
    kr = jnp.where(lane < MLA_ROPE, krgl_ref[0].astype(F32), 0.0)
    kr = _rope_lanes(kr, c, s1, s2).astype(BF16)
    for hd in range(N_HEADS):
        lo, mid, hi = hd * VAUG_W, hd * VAUG_W + LANES, (hd + 1) * VAUG_W
        mqo_ref[0, :, lo:mid] = (qm[:, lo:mid] * mla_qscale).astype(BF16)
        qr = _rope_lanes(qm[:, mid:hi], c, s1, s2) * mla_qscale
        mqo_ref[0, :, mid:hi] = qr.astype(BF16)
        mko_ref[0, :, lo:mid] = kv[:, lo:mid].astype(BF16)
        mko_ref[0, :, mid:hi] = kr
        mvo_ref[0, :, lo:mid] = kv[:, mid:hi].astype(BF16)
        mvo_ref[0, :, mid:hi] = ones_col


def _prep(proj3, tables, qnw, wqb, kvnw, wkvb):
    b, s, _ = proj3.shape
    tm = min(512, s)
    cos, s1, s2 = tables
    hw = N_HEADS * LANES
    aw = N_HEADS * VAUG_W
    tok = lambda cb: (lambda bi, i: (bi, i, cb))
    tab = pl.BlockSpec((tm, LANES), lambda bi, i: (i, 0))
    const = lambda shape: pl.BlockSpec(shape, lambda bi, i: (0, 0))
    out_sd = lambda w: jax.ShapeDtypeStruct((b, s, w), BF16)
    out_bs = lambda w: pl.BlockSpec((1, tm, w), lambda bi, i: (bi, i, 0))
    return pl.pallas_call(
        _prep_kernel,
        out_shape=(out_sd(hw), out_sd(hw), out_sd(aw), out_sd(aw), out_sd(aw), out_sd(aw)),
        grid=(b, s // tm),
        in_specs=[pl.BlockSpec((1, tm, hw), tok(COL_DIFF // hw)),
                  pl.BlockSpec((1, tm, hw), tok(COL_DIFF // hw + 1)),
                  pl.BlockSpec((1, tm, hw), tok(COL_DIFF // hw + 2)),
                  pl.BlockSpec((1, tm, hw), tok(COL_MLA // hw)),
                  pl.BlockSpec((1, tm, LANES), tok(COL_KRGL // LANES)),
                  tab, tab, tab,
                  const((1, MLA_Q_LORA)), const(wqb.shape),
                  const((1, MLA_KV_LORA)), const(wkvb.shape)],
        out_specs=(out_bs(hw), out_bs(hw), out_bs(aw), out_bs(aw), out_bs(aw), out_bs(aw)),
        compiler_params=_cparams(("parallel", "parallel"), VMEM_LIMIT),
        name="branch_prep",
    )(proj3, proj3, proj3, proj3, proj3, cos, s1, s2,
      qnw.reshape(1, -1), wqb, kvnw.reshape(1, -1), wkvb)


def _flash_kernel(*refs, n_maps, tk, lam_init):
    if n_maps == 2:
        q_ref, qnext_ref, kt_ref, v_ref, lam_ref, subw_ref, o_ref, m_scr, acc_scr, s_scr = refs
    else:
        q_ref, qnext_ref, kt_ref, v_ref, o_ref, m_scr, acc_scr, s_scr = refs
    nk = kt_ref.shape[2] // tk
    assert nk >= 2 and nk % 2 == 0

    def query_maps(q):
        if n_maps == 1:
            return (q,)
        lane = lax.broadcasted_iota(jnp.int32, q.shape, 1)
        zero = jnp.zeros_like(q)
        return (jnp.where(lane < DIFF_D, q, zero), jnp.where(lane >= DIFF_D, q, zero))

    qs = query_maps(q_ref[0])

    m_scr[...] = jnp.full(m_scr.shape, -jnp.inf, F32)
    acc_scr[...] = jnp.zeros(acc_scr.shape, F32)

    def scores(j, slot, q_maps=qs):
        kt = kt_ref[0, :, pl.ds(pl.multiple_of(j * tk, tk), tk)]
        for mi in range(n_maps):
            s_scr[slot, mi] = jnp.dot(q_maps[mi], kt, preferred_element_type=F32)

    def accumulate(j, slot):
        v = v_ref[0, pl.ds(pl.multiple_of(j * tk, tk), tk), :]
        for mi in range(n_maps):
            s = s_scr[slot, mi]
            m_prev = m_scr[mi]
            m_new = jnp.maximum(m_prev, jnp.max(s, axis=1, keepdims=True))
            alpha = jnp.exp2(m_prev - m_new)
            p = jnp.exp2(s - jnp.tile(m_new, (1, tk // LANES)))
            pv = jnp.dot(p.astype(BF16), v, preferred_element_type=F32)
            acc_scr[mi] = acc_scr[mi] * jnp.tile(alpha, (1, VAUG_W // LANES)) + pv
            m_scr[mi] = m_new

    unroll = min(FLASH_UNROLL, nk)
    assert nk % unroll == 0 and unroll % 2 == 0

    @pl.when(pl.program_id(2) == 0)
    def _():
        scores(0, 0)

    def body(i, carry):
        j0 = unroll * i
        for t in range(unroll):
            scores(j0 + t + 1, (t + 1) % 2)
            accumulate(j0 + t, t % 2)
        return carry

    lax.fori_loop(0, nk // unroll - 1, body, 0)
    j0 = nk - unroll
    for t in range(unroll):
        if t + 1 < unroll:
            scores(j0 + t + 1, (t + 1) % 2)
        else:
            scores(0, 0, query_maps(qnext_ref[0]))
        accumulate(j0 + t, t % 2)

    outs = []
    for mi in range(n_maps):
        acc = acc_scr[mi]
        outs.append(acc[:, :LANES] / acc[:, LANES:LANES + 1])
    if n_maps == 2:
        lq = lam_ref[...]
        lam = (jnp.exp(jnp.sum(lq[0:1] * lq[1:2], axis=1, keepdims=True))
               - jnp.exp(jnp.sum(lq[2:3] * lq[3:4], axis=1, keepdims=True)) + lam_init)
        o = outs[0] - lam * outs[1]
        y = o * lax.rsqrt(jnp.mean(o * o, axis=-1, keepdims=True) + EPS) * subw_ref[...]
        o = y * (1.0 - lam_init)
    else:
        o = outs[0]
    o_ref[0] = o.astype(o_ref.dtype)


def _flash(q, k, vaug, *, n_maps, lam_qk=None, subw=None, lam_init=0.0):
    b, s, qw_all = q.shape
    qw = qw_all // N_HEADS
    kw = k.shape[2] // N_HEADS
    kt = jnp.swapaxes(k, 1, 2)
    tq = min(FLASH_TQ, s)
    tk = min(FLASH_TK, s // 2)
    nq = s // tq
    in_specs = [pl.BlockSpec((1, tq, qw), lambda bi, hd, i: (bi, i, hd)),
                pl.BlockSpec((1, tq, qw), lambda bi, hd, i: (bi, jnp.minimum(i + 1, nq - 1), hd)),
                pl.BlockSpec((1, kw, s), lambda bi, hd, i: (bi, hd, 0)),
                pl.BlockSpec((1, s, VAUG_W), lambda bi, hd, i: (bi, 0, hd))]
    args = [q, q, kt, vaug]
    if n_maps == 2:
        in_specs += [pl.BlockSpec(lam_qk.shape, lambda bi, hd, i: (0, 0)),
                     pl.BlockSpec((1, LANES), lambda bi, hd, i: (0, 0))]
        args += [lam_qk, subw.reshape(1, LANES)]
    return pl.pallas_call(
        functools.partial(_flash_kernel, n_maps=n_maps, tk=tk, lam_init=lam_init),
        out_shape=jax.ShapeDtypeStruct((b, s, N_HEADS * LANES), BF16),
        grid=(b, N_HEADS, nq),
        in_specs=in_specs,
        out_specs=pl.BlockSpec((1, tq, LANES), lambda bi, hd, i: (bi, i, hd)),
        scratch_shapes=[pltpu.VMEM((n_maps, tq, LANES), F32),
                        pltpu.VMEM((n_maps, tq, VAUG_W), F32),
                        pltpu.VMEM((2, n_maps, tq, tk), F32)],
        compiler_params=_cparams(("parallel", "parallel", "arbitrary"), VMEM_LIMIT),
        name="diff_attention" if n_maps == 2 else "latent_attention",
    )(*args)


def _na_kernel(q_ref, *refs):
    k_refs, v_refs = refs[:NA_KV_PARTS], refs[NA_KV_PARTS:2 * NA_KV_PARTS]
    b_ref, o_ref = refs[2 * NA_KV_PARTS:]
    for hd in range(N_HEADS):
        sl = slice(hd * LANES, (hd + 1) * LANES)
        k = jnp.concatenate([r[0, :, sl] for r in k_refs], axis=0)
        v = jnp.concatenate([r[0, :, sl] for r in v_refs], axis=0)
        s = _nt_dot(q_ref[0, :, sl], k) * (NA_HEAD_DIM ** -0.5) + b_ref[0, hd]
        m = jnp.max(s, axis=1, keepdims=True)
        p = jnp.exp(s - m)
        l = jnp.sum(p, axis=1, keepdims=True)
        o = jnp.dot(p.astype(BF16), v, preferred_element_type=F32) / l
        o_ref[0, :, sl] = o.astype(o_ref.dtype)


def _na_bias_tiles(rpb):
    a = np.arange(NA_QROWS)
    w = np.arange(NA_KROWS)
    c = np.arange(GRID_W)
    cs = np.clip(c - WIN_C // 2, 0, GRID_W - WIN_C)
    col_ok = (c[None, :] >= cs[:, None]) & (c[None, :] < cs[:, None] + WIN_C)
    dc = c[None, :] - c[:, None] + (WIN_C - 1)
    sel_c = ((dc[..., None] == np.arange(2 * WIN_C - 1)) & col_ok[..., None]).astype(np.float32)
    half = WIN_R // 2
    sel_r, row_oks = [], []
    for off, wstart in ((0, np.maximum(a - half, 0)),
                        (half, a),
                        (NA_QROWS, np.minimum(a + half, NA_QROWS))):
        row_ok = (w[None, :] >= wstart[:, None]) & (w[None, :] < wstart[:, None] + WIN_R)
        dr = w[None, :] - a[:, None] - off + (WIN_R - 1)
        sel_r.append(((dr[..., None] == np.arange(2 * WIN_R - 1)) & row_ok[..., None]).astype(np.float32))
        row_oks.append(row_ok)
    sel_r = np.stack(sel_r)
    ok = np.stack(row_oks)[:, :, None, :, None] & col_ok[None, None, :, None, :]
    toep = jnp.einsum("qkc,lhdc->lhdqk", sel_c, rpb.astype(F32), precision=lax.Precision.HIGHEST)
    tiles = jnp.einsum("tawd,lhdqk->lthaqwk", sel_r, toep, precision=lax.Precision.HIGHEST)
    tiles = jnp.where(ok[None, :, None], tiles, NEG_BIG)
    return tiles.reshape(rpb.shape[0], 3, N_HEADS, NA_TQ, NA_TK)


def _na(proj3, bias_tiles):
    b, s, _ = proj3.shape
    rows = s // GRID_W
    assert rows % NA_QROWS == 0 and rows >= NA_KROWS
    nq = rows // NA_QROWS
    hw = N_HEADS * LANES
    part_rows = NA_KROWS // NA_KV_PARTS
    part = part_rows * GRID_W
    qb = COL_NA // hw

    def kv_spec(cb, t):
        def idx(i, bi):
            ks_row = jnp.clip(i * NA_QROWS - WIN_R // 2, 0, rows - NA_KROWS)
            return (bi, ks_row // part_rows + t, cb)
        return pl.BlockSpec((1, part, hw), idx)

    def bias_idx(i, bi):
        kind = jnp.where(i == 0, 0, jnp.where(i == nq - 1, 2, 1))
        return (kind, 0, 0, 0)

    return pl.pallas_call(
        _na_kernel,
        out_shape=jax.ShapeDtypeStruct((b, s, hw), BF16),
        grid=(nq, b),
        in_specs=([pl.BlockSpec((1, NA_TQ, hw), lambda i, bi: (bi, i, qb))]
                  + [kv_spec(qb + 1, t) for t in range(NA_KV_PARTS)]
                  + [kv_spec(qb + 2, t) for t in range(NA_KV_PARTS)]
                  + [pl.BlockSpec((1, N_HEADS, NA_TQ, NA_TK), bias_idx)]),
        out_specs=pl.BlockSpec((1, NA_TQ, hw), lambda i, bi: (bi, i, 0)),
        compiler_params=_cparams(("parallel", "parallel"), VMEM_LIMIT),
        name="neighborhood_attention",
    )(proj3, *([proj3] * (2 * NA_KV_PARTS)), bias_tiles)


def _gla_kernel(*refs, reverse):
    if reverse:
        q_ref, k_ref, v_ref, gl_ref, wg_ref, bg_ref, ofwd_ref, nw_ref, o_ref, st_scr = refs
    else:
        q_ref, k_ref, v_ref, gl_ref, wg_ref, bg_ref, o_ref, st_scr = refs
    rows_per_step, t = q_ref.shape[0], q_ref.shape[1]
    gw = N_HEADS * GLA_DK
    c_len = GLA_CHUNK

    @pl.when(pl.program_id(1) == 0)
    def _():
        st_scr[...] = jnp.zeros(st_scr.shape, F32)

    pos = lax.broadcasted_iota(jnp.int32, (t, gw), 0) & (c_len - 1)
    ref_i = c_len // 2 if reverse else c_len // 2 - 1
    last_i = 0 if reverse else c_len - 1
    n_chunks = t // c_len
    order = range(n_chunks - 1, -1, -1) if reverse else range(n_chunks)
    ri = lax.broadcasted_iota(jnp.int32, (t, t), 0)
    ci = lax.broadcasted_iota(jnp.int32, (t, t), 1)
    same_chunk = (ri // c_len) == (ci // c_len)
    intra = same_chunk & ((ci >= ri) if reverse else (ri >= ci))
    lane = lax.broadcasted_iota(jnp.int32, (t, LANES), 1)
    row_chunk = lax.broadcasted_iota(jnp.int32, (t, LANES), 0) // c_len

    def spread(x):
        return jnp.concatenate([jnp.where(row_chunk == n, x, 0.0) for n in range(n_chunks)], axis=1)

    def per_chunk_rows(x, i):
        return jnp.concatenate([jnp.broadcast_to(x[n * c_len + i:n * c_len + i + 1], (c_len, x.shape[1]))
                                for n in range(n_chunks)], axis=0)

    for bi in range(rows_per_step):
        gpre = jnp.dot(gl_ref[bi], wg_ref[...], preferred_element_type=F32) + bg_ref[...]
        g = (jnp.minimum(gpre, 0.0) - jnp.log1p(jnp.exp(-jnp.abs(gpre)))) * (1.0 / GLA_TAU)

        cum = g
        for sh in (1, 2, 4, 8, 16, 32):
            if reverse:
                cum = cum + jnp.where(pos < c_len - sh, pltpu.roll(cum, t - sh, 0), 0.0)
            else:
                cum = cum + jnp.where(pos >= sh, pltpu.roll(cum, sh, 0), 0.0)

        ref = per_chunk_rows(cum, ref_i)
        last = per_chunk_rows(cum, last_i)
        qf = q_ref[bi].astype(F32) * (GLA_DK ** -0.5)
        kf = k_ref[bi].astype(F32)
        qd = qf * jnp.exp(cum - ref)
        kd = (kf * jnp.exp(ref - cum)).astype(BF16)
        qe = qf * jnp.exp(cum)
        kl = kf * jnp.exp(last - cum)
        dec = jnp.exp(last)

        for pr in range(N_HEADS // 2):
            lsl = slice(pr * LANES, (pr + 1) * LANES)
            owns = (lane < GLA_DK, lane >= GLA_DK)
            qd_p, qe_p = qd[:, lsl], qe[:, lsl]
            lhs = jnp.concatenate([jnp.where(own, qd_p, 0.0) for own in owns], axis=0).astype(BF16)
            a_pair = _nt_dot(lhs, kd[:, lsl])
            kl_spread = spread(kl[:, lsl]).astype(BF16)
            for hh in range(2):
                hd = 2 * pr + hh
                vsl = slice(hd * GLA_DV, (hd + 1) * GLA_DV)
                v_h = v_ref[bi, :, vsl]
                a = jnp.where(intra, a_pair[hh * t:(hh + 1) * t], 0.0)
                o_h = jnp.dot(a.astype(BF16), v_h, preferred_element_type=F32)
                ds_all = _tn_dot(v_h, kl_spread)
                st = st_scr[bi, hd]
                before = [None] * n_chunks
                for n in order:
                    before[n] = st
                    dec_n = dec[n * c_len:n * c_len + 1, lsl]
                    st = st * dec_n + ds_all[:, n * LANES:(n + 1) * LANES]
                st_scr[bi, hd] = st
                st_cat = jnp.concatenate(before, axis=1).astype(BF16)
                qe_spread = spread(jnp.where(owns[hh], qe_p, 0.0)).astype(BF16)
                o_h = o_h + _nt_dot(qe_spread, st_cat)
                if reverse:
                    tot = o_h + ofwd_ref[bi, :, vsl]
                    y = tot * lax.rsqrt(jnp.mean(tot * tot, axis=-1, keepdims=True) + EPS) * nw_ref[...]
                    o_ref[bi, :, vsl] = y.astype(o_ref.dtype)
                else:
                    o_ref[bi, :, vsl] = o_h


def _gla_pass(proj3, wg, bg, reverse, o_fwd=None, norm_w=None):
    b, s, _ = proj3.shape
    t = min(GLA_T, s)
    nb = s // t
    gw = N_HEADS * GLA_DK
    vw = N_HEADS * GLA_DV
    blk = (lambda i: nb - 1 - i) if reverse else (lambda i: i)
    tok = lambda cb: (lambda bi, i: (bi, blk(i), cb))
    const = lambda shape: pl.BlockSpec(shape, lambda bi, i: (0, 0))
    rb = GLA_ROWS_PER_STEP
    assert b % rb == 0
    in_specs = [pl.BlockSpec((rb, t, gw), tok(COL_GLA // gw)),
                pl.BlockSpec((rb, t, gw), tok(COL_GLA // gw + 1)),
                pl.BlockSpec((rb, t, vw), tok((COL_GLA + 2 * gw) // vw)),
                pl.BlockSpec((rb, t, LANES), tok(COL_KRGL // LANES)),
                const((LANES, gw)), const((1, gw))]
    args = [proj3, proj3, proj3, proj3, wg, bg]
    if reverse:
        in_specs += [pl.BlockSpec((rb, t, vw), tok(0)), const((1, GLA_DV))]
        args += [o_fwd, norm_w.reshape(1, GLA_DV)]
    return pl.pallas_call(
        functools.partial(_gla_kernel, reverse=reverse),
        out_shape=jax.ShapeDtypeStruct((b, s, vw), BF16 if reverse else F32),
        grid=(b // rb, nb),
        in_specs=in_specs,
        out_specs=pl.BlockSpec((rb, t, vw), tok(0)),
        scratch_shapes=[pltpu.VMEM((rb, N_HEADS, GLA_DV, LANES), F32)],
        compiler_params=_cparams(("parallel", "arbitrary")),
        name="gla_backward" if reverse else "gla_forward",
    )(*args)


def _gla(proj3, w_gate_up, b_gate, norm_w):
    gw = N_HEADS * GLA_DK
    wgs = []
    for d in range(2):
        lo = GL_LANE0 + d * GLA_RANK
        wgs.append(jnp.zeros((LANES, gw), F32).at[lo:lo + GLA_RANK].set(w_gate_up[d]).astype(BF16))
    o_fwd = _gla_pass(proj3, wgs[0], b_gate[0].reshape(1, gw), False)
    return _gla_pass(proj3, wgs[1], b_gate[1].reshape(1, gw), True, o_fwd, norm_w)


def _merge_kernel(h_ref, oa_ref, ob_ref, oc_ref, od_ref, za_ref, zb_ref, zc_ref, zd_ref,
                  wg_ref, wb_ref, o_ref):
    tn = o_ref.shape[1]
    gates = jax.nn.sigmoid(jnp.dot(h_ref[...], wg_ref[0], preferred_element_type=F32))
    acc = None
    for i, (o_r, z_r) in enumerate(((oa_ref, za_ref), (ob_ref, zb_ref), (oc_ref, zc_ref), (od_ref, zd_ref))):
        z = z_r[...].astype(F32)
        u = (o_r[...].astype(F32) * (z * jax.nn.sigmoid(z))).astype(BF16)
        t = gates[:, i * tn:(i + 1) * tn] * jnp.dot(u, wb_ref[i], preferred_element_type=F32)
        acc = t if acc is None else acc + t
    o_ref[...] = acc.astype(o_ref.dtype)


def _merge(h, branches, proj, wg, wb):
    n, d = h.shape
    tm = min(512, n)
    tn = min(MERGE_TN, d)
    assert wg.shape == (d // tn, d, N_BRANCH * tn)
    zb0 = COL_Z // BRANCH_W
    tok = pl.BlockSpec((tm, BRANCH_W), lambda j, i: (i, 0))
    zspec = lambda k: pl.BlockSpec((tm, BRANCH_W), lambda j, i: (i, zb0 + k))
    return pl.pallas_call(
        _merge_kernel,
        out_shape=jax.ShapeDtypeStruct((n, d), BF16),
        grid=(d // tn, n // tm),
        in_specs=[pl.BlockSpec((tm, d), lambda j, i: (i, 0)), tok, tok, tok, tok,
                  zspec(0), zspec(1), zspec(2), zspec(3),
                  pl.BlockSpec((1, d, N_BRANCH * tn), lambda j, i: (j, 0, 0)),
                  pl.BlockSpec((N_BRANCH, BRANCH_W, tn), lambda j, i: (0, 0, j))],
        out_specs=pl.BlockSpec((tm, tn), lambda j, i: (i, j)),
        compiler_params=_cparams(("parallel", "parallel"), VMEM_LIMIT),
        name="branch_merge",
    )(h, *branches, proj, proj, proj, proj, wg, wb)


def _out_kernel(m_ref, w_ref, x_ref, pw_ref, *rest):
    y = jnp.dot(m_ref[...], w_ref[...], preferred_element_type=F32)
    y = y * lax.rsqrt(jnp.mean(y * y, axis=-1, keepdims=True) + EPS) * pw_ref[...]
    xn = x_ref[...] + y
    if len(rest) == 1:
        (xo_ref,) = rest
    else:
        nw_ref, xo_ref, ho_ref = rest
        hn = xn * lax.rsqrt(jnp.mean(xn * xn, axis=-1, keepdims=True) + EPS) * nw_ref[...]
        ho_ref[...] = hn.astype(ho_ref.dtype)
    xo_ref[...] = xn


def _out_proj(merged, w_out, x, post_w, next_pre_w=None):
    n, d = x.shape
    tm = min(512, n)
    tok = lambda: pl.BlockSpec((tm, d), lambda i: (i, 0))
    vec = lambda: pl.BlockSpec((1, d), lambda i: (0, 0))
    in_specs = [tok(), pl.BlockSpec((d, d), lambda i: (0, 0)), tok(), vec()]
    args = [merged, w_out, x, post_w.reshape(1, d)]
    out_shape = [jax.ShapeDtypeStruct((n, d), F32)]
    out_specs = [tok()]
    if next_pre_w is not None:
        in_specs.append(vec())
        args.append(next_pre_w.reshape(1, d))
        out_shape.append(jax.ShapeDtypeStruct((n, d), BF16))
        out_specs.append(tok())
    return pl.pallas_call(
        _out_kernel,
        out_shape=tuple(out_shape),
        grid=(n // tm,),
        in_specs=in_specs,
        out_specs=tuple(out_specs),
        compiler_params=_cparams(("parallel",), VMEM_LIMIT),
        name="out_proj",
    )(*args)


def _permute_w_in(w_in):
    depth, d, _ = w_in.shape
    na = w_in[:, :, :NA_COLS]
    diff = w_in[:, :, NA_COLS:NA_COLS + DIFF_COLS]
    g0 = NA_COLS + DIFF_COLS
    gla_qkv = w_in[:, :, g0:g0 + GLA_QKV_COLS]
    gl = w_in[:, :, g0 + GLA_QKV_COLS:g0 + GLA_COLS]
    m0 = g0 + GLA_COLS
    cq_ckv = w_in[:, :, m0:m0 + MLA_Q_LORA + MLA_KV_LORA]
    kr = w_in[:, :, m0 + MLA_Q_LORA + MLA_KV_LORA:m0 + MLA_COLS]
    z = w_in[:, :, m0 + MLA_COLS:]
    used = COL_KRGL + MLA_ROPE + 2 * GLA_RANK
    pad = jnp.zeros((depth, d, PROJ_COLS - used), w_in.dtype)
    return jnp.concatenate([na, diff, gla_qkv, cq_ckv, z, kr, gl, pad], axis=-1).astype(BF16)


def _tile_w_gate(w_gate):
    depth, nb, d, _ = w_gate.shape
    tn = min(MERGE_TN, d)
    w = w_gate.astype(BF16).reshape(depth, nb, d, d // tn, tn)
    return jnp.transpose(w, (0, 3, 2, 1, 4)).reshape(depth, d // tn, d, nb * tn)


def _pad_w_qb(w_qb):
    depth, r, _ = w_qb.shape
    w = w_qb.reshape(depth, r, N_HEADS, MLA_NOPE + MLA_ROPE)
    w = jnp.pad(w, ((0, 0), (0, 0), (0, 0), (0, VAUG_W - MLA_NOPE - MLA_ROPE)))
    return w.reshape(depth, r, N_HEADS * VAUG_W).astype(BF16)


def _rope_tables(s):
    half = DIFF_D // 2
    inv = ROPE_THETA ** (-jnp.arange(0, DIFF_D, 2, dtype=F32) / DIFF_D)
    ang = jnp.arange(s, dtype=F32)[:, None] * inv[None, :]
    cos, sin = jnp.cos(ang), jnp.sin(ang)
    zero = jnp.zeros_like(sin)
    reps = LANES // DIFF_D
    c = jnp.tile(jnp.concatenate([cos, cos], axis=1), (1, reps))
    s1 = jnp.tile(jnp.concatenate([-sin, zero], axis=1), (1, reps))
    s2 = jnp.tile(jnp.concatenate([zero, sin], axis=1), (1, reps))
    assert half * 2 * reps == LANES
    return c, s1, s2


def _trunk(x, p):
    b, s, d = x.shape
    n = b * s
    tables = _rope_tables(s)
    xf = x.reshape(n, d)
    h = _rmsnorm(xf, p["pre_norm_w"][0])
    for l in range(DEPTH):
        lam_init = 0.8 - 0.6 * math.exp(-0.3 * l)
        proj = _in_proj(h, p["w_in"][l])
        proj3 = proj.reshape(b, s, PROJ_COLS)
        dq, dk, dva, mq, mk, mva = _prep(proj3, tables, p["mla_q_norm_w"][l], p["mla_w_qb"][l],
                                         p["mla_kv_norm_w"][l], p["mla_w_kvb"][l])
        o_a = _na(proj3, p["na_bias"][l])
        o_b = _flash(dq, dk, dva, n_maps=2, lam_qk=p["diff_lambda_qk"][l],
                     subw=p["diff_subln_w"][l], lam_init=lam_init)
        o_c = _gla(proj3, p["gla_w_gate_up"][l], p["gla_b_gate"][l], p["gla_norm_w"][l])
        o_d = _flash(mq, mk, mva, n_maps=1)
        branches = [o.reshape(n, BRANCH_W) for o in (o_a, o_b, o_c, o_d)]
        merged = _merge(h, branches, proj, p["w_gate"][l], p["w_branch"][l])
        if l + 1 < DEPTH:
            xf, h = _out_proj(merged, p["w_out"][l], xf, p["post_norm_w"][l], p["pre_norm_w"][l + 1])
        else:
            (xf,) = _out_proj(merged, p["w_out"][l], xf, p["post_norm_w"][l])
    return xf.reshape(b, s, d)


def kernel(x_prompt, x_sample, pre_norm_w, w_in, na_rpb, diff_lambda_qk, diff_subln_w, gla_w_gate_up, gla_b_gate, gla_norm_w, mla_q_norm_w, mla_w_qb, mla_kv_norm_w, mla_w_kvb, w_gate, w_branch, w_out, post_norm_w):
    p = {
        "pre_norm_w": pre_norm_w,
        "w_in": _permute_w_in(w_in),
        "na_bias": _na_bias_tiles(na_rpb),
        "diff_lambda_qk": diff_lambda_qk,
        "diff_subln_w": diff_subln_w,
        "gla_w_gate_up": gla_w_gate_up,
        "gla_b_gate": gla_b_gate,
        "gla_norm_w": gla_norm_w,
        "mla_q_norm_w": mla_q_norm_w,
        "mla_w_qb": _pad_w_qb(mla_w_qb),
        "mla_kv_norm_w": mla_kv_norm_w,
        "mla_w_kvb": mla_w_kvb.astype(BF16),
        "w_gate": _tile_w_gate(w_gate),
        "w_branch": w_branch.astype(BF16),
        "w_out": w_out.astype(BF16),
        "post_norm_w": post_norm_w,
    }
    return (_trunk(x_prompt, p), _trunk(x_sample, p))
```

```python
import functools
import math

import jax
import jax.numpy as jnp
import numpy as np
from jax import lax
from jax.experimental import pallas as pl
from jax.experimental.pallas import tpu as pltpu

F32 = jnp.float32
BF16 = jnp.bfloat16

DEPTH = 4
GRID_W = 64
N_BRANCH = 4
BRANCH_W = 512
N_HEADS = 4
NA_HEAD_DIM = 128
WIN_R = 8
WIN_C = 16
DIFF_D = 64
GLA_DK = 64
GLA_DV = 128
GLA_RANK = 16
GLA_TAU = 16.0
GLA_CHUNK = 64
MLA_Q_LORA = 384
MLA_KV_LORA = 128
MLA_NOPE = 128
MLA_ROPE = 64
MLA_V = 128
ROPE_THETA = 10000.0
EPS = 1e-6
LOG2E = math.log2(math.e)

NA_COLS = 3 * N_HEADS * NA_HEAD_DIM
DIFF_COLS = 3 * N_HEADS * 2 * DIFF_D
GLA_QKV_COLS = 2 * N_HEADS * GLA_DK + N_HEADS * GLA_DV
GLA_COLS = GLA_QKV_COLS + 2 * GLA_RANK
MLA_COLS = MLA_Q_LORA + MLA_KV_LORA + MLA_ROPE
Z_COLS = N_BRANCH * BRANCH_W

LANES = 128
V7X_VMEM_BYTES = 64 * 1024 * 1024
VMEM_LIMIT = 56 * 1024 * 1024
MIB = 1024 * 1024

COL_NA = 0
COL_DIFF = COL_NA + NA_COLS
COL_GLA = COL_DIFF + DIFF_COLS
COL_MLA = COL_GLA + GLA_QKV_COLS
COL_Z = COL_MLA + MLA_Q_LORA + MLA_KV_LORA
COL_KRGL = COL_Z + Z_COLS
PROJ_TN = 1152
PROJ_COLS = 6 * PROJ_TN
GL_LANE0 = MLA_ROPE

NA_QROWS = 8
NA_KROWS = 16
NA_TQ = NA_QROWS * GRID_W
NA_TK = NA_KROWS * GRID_W
NA_KV_PARTS = 4
NEG_BIG = -1e30

VAUG_W = 2 * LANES
FLASH_TQ = 1024
FLASH_TK = 1024
FLASH_UNROLL = 2
MERGE_TN = 512
GLA_T = 256
GLA_ROWS_PER_STEP = 2


def _cparams(sem, vmem=None):
    return pltpu.CompilerParams(dimension_semantics=sem, vmem_limit_bytes=vmem)


def _nt_dot(a, b):
    return lax.dot_general(a, b, (((1,), (1,)), ((), ())), preferred_element_type=F32)


def _tn_dot(a, b):
    return lax.dot_general(a, b, (((0,), (0,)), ((), ())), preferred_element_type=F32)


def _rmsnorm_kernel(x_ref, w_ref, o_ref):
    x = x_ref[...]
    y = x * lax.rsqrt(jnp.mean(x * x, axis=-1, keepdims=True) + EPS)
    o_ref[...] = (y * w_ref[...]).astype(o_ref.dtype)


def _rmsnorm(x, w):
    n, d = x.shape
    tm = min(512, n)
    return pl.pallas_call(
        _rmsnorm_kernel,
        out_shape=jax.ShapeDtypeStruct((n, d), BF16),
        grid=(n // tm,),
        in_specs=[pl.BlockSpec((tm, d), lambda i: (i, 0)),
                  pl.BlockSpec((1, d), lambda i: (0, 0))],
        out_specs=pl.BlockSpec((tm, d), lambda i: (i, 0)),
        compiler_params=_cparams(("parallel",)),
        name="pre_norm",
    )(x, w.reshape(1, d))


def _matmul_kernel(a_ref, b_ref, o_ref):
    o_ref[...] = jnp.dot(a_ref[...], b_ref[...], preferred_element_type=F32).astype(o_ref.dtype)


def _in_proj(h, w, l):
    n, d = h.shape
    c = w.shape[2]
    tm = min(1024, n)
    tn = PROJ_TN
    return pl.pallas_call(
        _matmul_kernel,
        out_shape=jax.ShapeDtypeStruct((n, c), BF16),
        grid=(n // tm, c // tn),
        in_specs=[pl.BlockSpec((tm, d), lambda i, j: (i, 0)),
                  pl.BlockSpec((None, d, tn), lambda i, j: (l, 0, j))],
        out_specs=pl.BlockSpec((tm, tn), lambda i, j: (i, j)),
        compiler_params=_cparams(("parallel", "parallel"), VMEM_LIMIT),
        name="in_proj",
    )(h, w)


def _rope_lanes(x, c, s1, s2):
    return x * c + pltpu.roll(x, LANES - 32, 1) * s1 + pltpu.roll(x, 32, 1) * s2


def _prep_kernel(dq_ref, dk_ref, dv_ref, cqkv_ref, krgl_ref, cos_ref, s1_ref, s2_ref,
                 qnw_ref, wqb_ref, kvnw_ref, wkvb_ref,
                 dqo_ref, dko_ref, dvo_ref, mqo_ref, mko_ref, mvo_ref):
    c, s1, s2 = cos_ref[...], s1_ref[...], s2_ref[...]
    t = c.shape[0]
    lane = lax.broadcasted_iota(jnp.int32, (t, LANES), 1)
    ones_col = jnp.where(lane == 0, 1.0, 0.0).astype(BF16)
    diff_qscale = (DIFF_D ** -0.5) * LOG2E
    mla_qscale = ((MLA_NOPE + MLA_ROPE) ** -0.5) * LOG2E

    for hd in range(N_HEADS):
        sl = slice(hd * LANES, (hd + 1) * LANES)
        q = _rope_lanes(dq_ref[0, :, sl].astype(F32), c, s1, s2) * diff_qscale
        dqo_ref[0, :, sl] = q.astype(BF16)
        k = _rope_lanes(dk_ref[0, :, sl].astype(F32), c, s1, s2)
        dko_ref[0, :, sl] = k.astype(BF16)
        dvo_ref[0, :, hd * VAUG_W:hd * VAUG_W + LANES] = dv_ref[0, :, sl]
        dvo_ref[0, :, hd * VAUG_W + LANES:(hd + 1) * VAUG_W] = ones_col

    cq = cqkv_ref[0, :, :MLA_Q_LORA].astype(F32)
    cqn = cq * lax.rsqrt(jnp.mean(cq * cq, axis=-1, keepdims=True) + EPS) * qnw_ref[...]
    qm = jnp.dot(cqn.astype(BF16), wqb_ref[...], preferred_element_type=F32)
    ckv = cqkv_ref[0, :, MLA_Q_LORA:].astype(F32)
    ckvn = ckv * lax.rsqrt(jnp.mean(ckv * ckv, axis=-1, keepdims=True) + EPS) * kvnw_ref[...]
    kv = jnp.dot(ckvn.astype(BF16), wkvb_ref[...], preferred_element_type=F32)
    kr = jnp.where(lane < MLA_ROPE, krgl_ref[0].astype(F32), 0.0)
    kr = _rope_lanes(kr, c, s1, s2).astype(BF16)
    for hd in range(N_HEADS):
        lo, mid, hi = hd * VAUG_W, hd * VAUG_W + LANES, (hd + 1) * VAUG_W
        mqo_ref[0, :, lo:mid] = (qm[:, lo:mid] * mla_qscale).astype(BF16)
        qr = _rope_lanes(qm[:, mid:hi], c, s1, s2) * mla_qscale
        mqo_ref[0, :, mid:hi] = qr.astype(BF16)
        mko_ref[0, :, lo:mid] = kv[:, lo:mid].astype(BF16)
        mko_ref[0, :, mid:hi] = kr
        mvo_ref[0, :, lo:mid] = kv[:, mid:hi].astype(BF16)
        mvo_ref[0, :, mid:hi] = ones_col


def _prep(proj3, tables, qnw, wqb, kvnw, wkvb):
    b, s, _ = proj3.shape
    tm = min(512, s)
    cos, s1, s2 = tables
    hw = N_HEADS * LANES
    aw = N_HEADS * VAUG_W
    tok = lambda cb: (lambda bi, i: (bi, i, cb))
    tab = pl.BlockSpec((tm, LANES), lambda bi, i: (i, 0))
    const = lambda shape: pl.BlockSpec(shape, lambda bi, i: (0, 0))
    out_sd = lambda w: jax.ShapeDtypeStruct((b, s, w), BF16)
    out_bs = lambda w: pl.BlockSpec((1, tm, w), lambda bi, i: (bi, i, 0))
    return pl.pallas_call(
        _prep_kernel,
        out_shape=(out_sd(hw), out_sd(hw), out_sd(aw), out_sd(aw), out_sd(aw), out_sd(aw)),
        grid=(b, s // tm),
        in_specs=[pl.BlockSpec((1, tm, hw), tok(COL_DIFF // hw)),
                  pl.BlockSpec((1, tm, hw), tok(COL_DIFF // hw + 1)),
                  pl.BlockSpec((1, tm, hw), tok(COL_DIFF // hw + 2)),
                  pl.BlockSpec((1, tm, hw), tok(COL_MLA // hw)),
                  pl.BlockSpec((1, tm, LANES), tok(COL_KRGL // LANES)),
                  tab, tab, tab,
                  const((1, MLA_Q_LORA)), const(wqb.shape),
                  const((1, MLA_KV_LORA)), const(wkvb.shape)],
        out_specs=(out_bs(hw), out_bs(hw), out_bs(aw), out_bs(aw), out_bs(aw), out_bs(aw)),
        compiler_params=_cparams(("parallel", "parallel"), VMEM_LIMIT),
        name="branch_prep",
    )(proj3, proj3, proj3, proj3, proj3, cos, s1, s2,
      qnw.reshape(1, -1), wqb, kvnw.reshape(1, -1), wkvb)


def _flash_kernel(*refs, n_maps, tk, lam_init):
    if n_maps == 2:
        q_ref, kt_ref, v_ref, lam_ref, subw_ref, o_ref, m_scr, acc_scr, s_scr = refs
    else:
        q_ref, kt_ref, v_ref, o_ref, m_scr, acc_scr, s_scr = refs
    nk = kt_ref.shape[2] // tk
    assert nk >= 2 and nk % 2 == 0
    q = q_ref[0]
    if n_maps == 2:
        lane = lax.broadcasted_iota(jnp.int32, q.shape, 1)
        zero = jnp.zeros_like(q)
        qs = (jnp.where(lane < DIFF_D, q, zero), jnp.where(lane >= DIFF_D, q, zero))
    else:
        qs = (q,)

    m_scr[...] = jnp.full(m_scr.shape, -jnp.inf, F32)
    acc_scr[...] = jnp.zeros(acc_scr.shape, F32)

    def scores(j, slot):
        kt = kt_ref[0, :, pl.ds(pl.multiple_of(j * tk, tk), tk)]
        for mi in range(n_maps):
            s_scr[slot, mi] = jnp.dot(qs[mi], kt, preferred_element_type=F32)

    def accumulate(j, slot):
        v = v_ref[0, pl.ds(pl.multiple_of(j * tk, tk), tk), :]
        for mi in range(n_maps):
            s = s_scr[slot, mi]
            m_prev = m_scr[mi]
            m_new = jnp.maximum(m_prev, jnp.max(s, axis=1, keepdims=True))
            alpha = jnp.exp2(m_prev - m_new)
            p = jnp.exp2(s - jnp.tile(m_new, (1, tk // LANES)))
            pv = jnp.dot(p.astype(BF16), v, preferred_element_type=F32)
            acc_scr[mi] = acc_scr[mi] * jnp.tile(alpha, (1, VAUG_W // LANES)) + pv
            m_scr[mi] = m_new

    unroll = min(FLASH_UNROLL, nk)
    assert nk % unroll == 0 and unroll % 2 == 0
    scores(0, 0)

    def body(i, carry):
        j0 = unroll * i
        for t in range(unroll):
            scores(j0 + t + 1, (t + 1) % 2)
            accumulate(j0 + t, t % 2)
        return carry

    lax.fori_loop(0, nk // unroll - 1, body, 0)
    j0 = nk - unroll
    for t in range(unroll):
        if t + 1 < unroll:
            scores(j0 + t + 1, (t + 1) % 2)
        accumulate(j0 + t, t % 2)

    outs = []
    for mi in range(n_maps):
        acc = acc_scr[mi]
        outs.append(acc[:, :LANES] / acc[:, LANES:LANES + 1])
    if n_maps == 2:
        lq = lam_ref[...]
        lam = (jnp.exp(jnp.sum(lq[0:1] * lq[1:2], axis=1, keepdims=True))
               - jnp.exp(jnp.sum(lq[2:3] * lq[3:4], axis=1, keepdims=True)) + lam_init)
        o = outs[0] - lam * outs[1]
        y = o * lax.rsqrt(jnp.mean(o * o, axis=-1, keepdims=True) + EPS) * subw_ref[...]
        o = y * (1.0 - lam_init)
    else:
        o = outs[0]
    o_ref[0] = o.astype(o_ref.dtype)


def _flash(q, k, vaug, *, n_maps, lam_qk=None, subw=None, lam_init=0.0):
    b, s, qw_all = q.shape
    qw = qw_all // N_HEADS
    kw = k.shape[2] // N_HEADS
    kt = jnp.swapaxes(k, 1, 2)
    tq = min(FLASH_TQ, s)
    tk = min(FLASH_TK, s // 2)
    in_specs = [pl.BlockSpec((1, tq, qw), lambda bi, hd, i: (bi, i, hd)),
                pl.BlockSpec((1, kw, s), lambda bi, hd, i: (bi, hd, 0)),
                pl.BlockSpec((1, s, VAUG_W), lambda bi, hd, i: (bi, 0, hd))]
    args = [q, kt, vaug]
    if n_maps == 2:
        in_specs += [pl.BlockSpec(lam_qk.shape, lambda bi, hd, i: (0, 0)),
                     pl.BlockSpec((1, LANES), lambda bi, hd, i: (0, 0))]
        args += [lam_qk, subw.reshape(1, LANES)]
    return pl.pallas_call(
        functools.partial(_flash_kernel, n_maps=n_maps, tk=tk, lam_init=lam_init),
        out_shape=jax.ShapeDtypeStruct((b, s, N_HEADS * LANES), BF16),
        grid=(b, N_HEADS, s // tq),
        in_specs=in_specs,
        out_specs=pl.BlockSpec((1, tq, LANES), lambda bi, hd, i: (bi, i, hd)),
        scratch_shapes=[pltpu.VMEM((n_maps, tq, LANES), F32),
                        pltpu.VMEM((n_maps, tq, VAUG_W), F32),
                        pltpu.VMEM((2, n_maps, tq, tk), F32)],
        compiler_params=_cparams(("parallel", "parallel", "parallel"), VMEM_LIMIT),
        name="diff_attention" if n_maps == 2 else "latent_attention",
    )(*args)


def _na_kernel(q_ref, *refs):
    k_refs, v_refs = refs[:NA_KV_PARTS], refs[NA_KV_PARTS:2 * NA_KV_PARTS]
    b_ref, o_ref = refs[2 * NA_KV_PARTS:]
    for hd in range(N_HEADS):
        sl = slice(hd * LANES, (hd + 1) * LANES)
        k = jnp.concatenate([r[0, :, sl] for r in k_refs], axis=0)
        v = jnp.concatenate([r[0, :, sl] for r in v_refs], axis=0)
        s = _nt_dot(q_ref[0, :, sl], k) * (NA_HEAD_DIM ** -0.5) + b_ref[0, hd].astype(F32)
        m = jnp.max(s, axis=1, keepdims=True)
        p = jnp.exp(s - m)
        l = jnp.sum(p, axis=1, keepdims=True)
        o = jnp.dot(p.astype(BF16), v, preferred_element_type=F32) / l
        o_ref[0, :, sl] = o.astype(o_ref.dtype)


def _na_bias_tiles(rpb):
    a = np.arange(NA_QROWS)
    w = np.arange(NA_KROWS)
    c = np.arange(GRID_W)
    cs = np.clip(c - WIN_C // 2, 0, GRID_W - WIN_C)
    col_ok = (c[None, :] >= cs[:, None]) & (c[None, :] < cs[:, None] + WIN_C)
    dc = c[None, :] - c[:, None] + (WIN_C - 1)
    sel_c = ((dc[..., None] == np.arange(2 * WIN_C - 1)) & col_ok[..., None]).astype(np.float32)
    half = WIN_R // 2
    sel_r, row_oks = [], []
    for off, wstart in ((0, np.maximum(a - half, 0)),
                        (half, a),
                        (NA_QROWS, np.minimum(a + half, NA_QROWS))):
        row_ok = (w[None, :] >= wstart[:, None]) & (w[None, :] < wstart[:, None] + WIN_R)
        dr = w[None, :] - a[:, None] - off + (WIN_R - 1)
        sel_r.append(((dr[..., None] == np.arange(2 * WIN_R - 1)) & row_ok[..., None]).astype(np.float32))
        row_oks.append(row_ok)
    sel_r = np.stack(sel_r)
    ok = np.stack(row_oks)[:, :, None, :, None] & col_ok[None, None, :, None, :]
    toep = jnp.einsum("qkc,lhdc->lhdqk", sel_c, rpb.astype(F32), precision=lax.Precision.HIGHEST)
    tiles = jnp.einsum("tawd,lhdqk->lthaqwk", sel_r, toep, precision=lax.Precision.HIGHEST)
    tiles = jnp.where(ok[None, :, None], tiles, NEG_BIG).astype(BF16)
    return tiles.reshape(rpb.shape[0], 3, N_HEADS, NA_TQ, NA_TK)


def _na(proj3, bias_tiles, l):
    b, s, _ = proj3.shape
    rows = s // GRID_W
    assert rows % NA_QROWS == 0 and rows >= NA_KROWS
    nq = rows // NA_QROWS
    hw = N_HEADS * LANES
    part_rows = NA_KROWS // NA_KV_PARTS
    part = part_rows * GRID_W
    qb = COL_NA // hw

    def kv_spec(cb, t):
        def idx(i, bi):
            ks_row = jnp.clip(i * NA_QROWS - WIN_R // 2, 0, rows - NA_KROWS)
            return (bi, ks_row // part_rows + t, cb)
        return pl.BlockSpec((1, part, hw), idx)

    def bias_idx(i, bi):
        kind = jnp.where(i == 0, 0, jnp.where(i == nq - 1, 2, 1))
        return (l, kind, 0, 0, 0)

    return pl.pallas_call(
        _na_kernel,
        out_shape=jax.ShapeDtypeStruct((b, s, hw), BF16),
        grid=(nq, b),
        in_specs=([pl.BlockSpec((1, NA_TQ, hw), lambda i, bi: (bi, i, qb))]
                  + [kv_spec(qb + 1, t) for t in range(NA_KV_PARTS)]
                  + [kv_spec(qb + 2, t) for t in range(NA_KV_PARTS)]
                  + [pl.BlockSpec((None, 1, N_HEADS, NA_TQ, NA_TK), bias_idx)]),
        out_specs=pl.BlockSpec((1, NA_TQ, hw), lambda i, bi: (bi, i, 0)),
        compiler_params=_cparams(("parallel", "parallel"), VMEM_LIMIT),
        name="neighborhood_attention",
    )(proj3, *([proj3] * (2 * NA_KV_PARTS)), bias_tiles)


def _gla_kernel(*refs, reverse):
    if reverse:
        q_ref, k_ref, v_ref, gl_ref, wg_ref, bg_ref, ofwd_ref, nw_ref, o_ref, st_scr = refs
    else:
        q_ref, k_ref, v_ref, gl_ref, wg_ref, bg_ref, o_ref, st_scr = refs
    rows_per_step, t = q_ref.shape[0], q_ref.shape[1]
    gw = N_HEADS * GLA_DK
    c_len = GLA_CHUNK

    @pl.when(pl.program_id(1) == 0)
    def _():
        st_scr[...] = jnp.zeros(st_scr.shape, F32)

    pos = lax.broadcasted_iota(jnp.int32, (t, gw), 0) & (c_len - 1)
    ref_i = c_len // 2 if reverse else c_len // 2 - 1
    last_i = 0 if reverse else c_len - 1
    n_chunks = t // c_len
    order = range(n_chunks - 1, -1, -1) if reverse else range(n_chunks)
    ri = lax.broadcasted_iota(jnp.int32, (t, t), 0)
    ci = lax.broadcasted_iota(jnp.int32, (t, t), 1)
    same_chunk = (ri // c_len) == (ci // c_len)
    intra = same_chunk & ((ci >= ri) if reverse else (ri >= ci))
    lane = lax.broadcasted_iota(jnp.int32, (t, LANES), 1)
    row_chunk = lax.broadcasted_iota(jnp.int32, (t, LANES), 0) // c_len

    def spread(x):
        return jnp.concatenate([jnp.where(row_chunk == n, x, 0.0) for n in range(n_chunks)], axis=1)

    def per_chunk_rows(x, i):
        return jnp.concatenate([jnp.broadcast_to(x[n * c_len + i:n * c_len + i + 1], (c_len, x.shape[1]))
                                for n in range(n_chunks)], axis=0)

    for bi in range(rows_per_step):
        gpre = jnp.dot(gl_ref[bi], wg_ref[...], preferred_element_type=F32) + bg_ref[...]
        g = (jnp.minimum(gpre, 0.0) - jnp.log1p(jnp.exp(-jnp.abs(gpre)))) * (1.0 / GLA_TAU)

        cum = g
        for sh in (1, 2, 4, 8, 16, 32):
            if reverse:
                cum = cum + jnp.where(pos < c_len - sh, pltpu.roll(cum, t - sh, 0), 0.0)
            else:
                cum = cum + jnp.where(pos >= sh, pltpu.roll(cum, sh, 0), 0.0)

        ref = per_chunk_rows(cum, ref_i)
        last = per_chunk_rows(cum, last_i)
        qf = q_ref[bi].astype(F32) * (GLA_DK ** -0.5)
        kf = k_ref[bi].astype(F32)
        qd = qf * jnp.exp(cum - ref)
        kd = (kf * jnp.exp(ref - cum)).astype(BF16)
        qe = qf * jnp.exp(cum)
        kl = kf * jnp.exp(last - cum)
        dec = jnp.exp(last)

        for pr in range(N_HEADS // 2):
            lsl = slice(pr * LANES, (pr + 1) * LANES)
            owns = (lane < GLA_DK, lane >= GLA_DK)
            qd_p, qe_p = qd[:, lsl], qe[:, lsl]
            lhs = jnp.concatenate([jnp.where(own, qd_p, 0.0) for own in owns], axis=0).astype(BF16)
            a_pair = _nt_dot(lhs, kd[:, lsl])
            kl_spread = spread(kl[:, lsl]).astype(BF16)
            for hh in range(2):
                hd = 2 * pr + hh
                vsl = slice(hd * GLA_DV, (hd + 1) * GLA_DV)
                v_h = v_ref[bi, :, vsl]
                a = jnp.where(intra, a_pair[hh * t:(hh + 1) * t], 0.0)
                o_h = jnp.dot(a.astype(BF16), v_h, preferred_element_type=F32)
                ds_all = _tn_dot(v_h, kl_spread)
                st = st_scr[bi, hd]
                before = [None] * n_chunks
                for n in order:
                    before[n] = st
                    dec_n = dec[n * c_len:n * c_len + 1, lsl]
                    st = st * dec_n + ds_all[:, n * LANES:(n + 1) * LANES]
                st_scr[bi, hd] = st
                st_cat = jnp.concatenate(before, axis=1).astype(BF16)
                qe_spread = spread(jnp.where(owns[hh], qe_p, 0.0)).astype(BF16)
                o_h = o_h + _nt_dot(qe_spread, st_cat)
                if reverse:
                    tot = o_h + ofwd_ref[bi, :, vsl]
                    y = tot * lax.rsqrt(jnp.mean(tot * tot, axis=-1, keepdims=True) + EPS) * nw_ref[...]
                    o_ref[bi, :, vsl] = y.astype(o_ref.dtype)
                else:
                    o_ref[bi, :, vsl] = o_h


def _gla_pass(proj3, wg, bg, reverse, o_fwd=None, norm_w=None):
    b, s, _ = proj3.shape
    t = min(GLA_T, s)
    nb = s // t
    gw = N_HEADS * GLA_DK
    vw = N_HEADS * GLA_DV
    blk = (lambda i: nb - 1 - i) if reverse else (lambda i: i)
    tok = lambda cb: (lambda bi, i: (bi, blk(i), cb))
    const = lambda shape: pl.BlockSpec(shape, lambda bi, i: (0, 0))
    rb = GLA_ROWS_PER_STEP
    assert b % rb == 0
    in_specs = [pl.BlockSpec((rb, t, gw), tok(COL_GLA // gw)),
                pl.BlockSpec((rb, t, gw), tok(COL_GLA // gw + 1)),
                pl.BlockSpec((rb, t, vw), tok((COL_GLA + 2 * gw) // vw)),
                pl.BlockSpec((rb, t, LANES), tok(COL_KRGL // LANES)),
                const((LANES, gw)), const((1, gw))]
    args = [proj3, proj3, proj3, proj3, wg, bg]
    if reverse:
        in_specs += [pl.BlockSpec((rb, t, vw), tok(0)), const((1, GLA_DV))]
        args += [o_fwd, norm_w.reshape(1, GLA_DV)]
    return pl.pallas_call(
        functools.partial(_gla_kernel, reverse=reverse),
        out_shape=jax.ShapeDtypeStruct((b, s, vw), BF16 if reverse else F32),
        grid=(b // rb, nb),
        in_specs=in_specs,
        out_specs=pl.BlockSpec((rb, t, vw), tok(0)),
        scratch_shapes=[pltpu.VMEM((rb, N_HEADS, GLA_DV, LANES), F32)],
        compiler_params=_cparams(("parallel", "arbitrary")),
        name="gla_backward" if reverse else "gla_forward",
    )(*args)


def _gla(proj3, w_gate_up, b_gate, norm_w):
    gw = N_HEADS * GLA_DK
    wgs = []
    for d in range(2):
        lo = GL_LANE0 + d * GLA_RANK
        wgs.append(jnp.zeros((LANES, gw), F32).at[lo:lo + GLA_RANK].set(w_gate_up[d]).astype(BF16))
    o_fwd = _gla_pass(proj3, wgs[0], b_gate[0].reshape(1, gw), False)
    return _gla_pass(proj3, wgs[1], b_gate[1].reshape(1, gw), True, o_fwd, norm_w)


def _merge_kernel(h_ref, oa_ref, ob_ref, oc_ref, od_ref, za_ref, zb_ref, zc_ref, zd_ref,
                  wg_ref, wb_ref, o_ref):
    h = h_ref[...]
    acc = None
    for i, (o_r, z_r) in enumerate(((oa_ref, za_ref), (ob_ref, zb_ref), (oc_ref, zc_ref), (od_ref, zd_ref))):
        z = z_r[...].astype(F32)
        u = (o_r[...].astype(F32) * (z * jax.nn.sigmoid(z))).astype(BF16)
        t = jnp.dot(u, wb_ref[i], preferred_element_type=F32)
        gate = jax.nn.sigmoid(jnp.dot(h, wg_ref[i], preferred_element_type=F32))
        acc = gate * t if acc is None else acc + gate * t
    o_ref[...] = acc.astype(o_ref.dtype)


def _merge(h, branches, proj, wg, wb, l):
    n, d = h.shape
    tm = min(512, n)
    tn = min(MERGE_TN, d)
    zb0 = COL_Z // BRANCH_W
    tok = pl.BlockSpec((tm, BRANCH_W), lambda j, i: (i, 0))
    zspec = lambda k: pl.BlockSpec((tm, BRANCH_W), lambda j, i: (i, zb0 + k))
    return pl.pallas_call(
        _merge_kernel,
        out_shape=jax.ShapeDtypeStruct((n, d), BF16),
        grid=(d // tn, n // tm),
        in_specs=[pl.BlockSpec((tm, d), lambda j, i: (i, 0)), tok, tok, tok, tok,
                  zspec(0), zspec(1), zspec(2), zspec(3),
                  pl.BlockSpec((None, N_BRANCH, d, tn), lambda j, i: (l, 0, 0, j)),
                  pl.BlockSpec((None, N_BRANCH, BRANCH_W, tn), lambda j, i: (l, 0, 0, j))],
        out_specs=pl.BlockSpec((tm, tn), lambda j, i: (i, j)),
        compiler_params=_cparams(("parallel", "parallel"), VMEM_LIMIT),
        name="branch_merge",
    )(h, *branches, proj, proj, proj, proj, wg, wb)


def _out_kernel(m_ref, w_ref, x_ref, pw_ref, *rest):
    y = jnp.dot(m_ref[...], w_ref[...], preferred_element_type=F32)
    y = y * lax.rsqrt(jnp.mean(y * y, axis=-1, keepdims=True) + EPS) * pw_ref[...]
    xn = x_ref[...] + y
    if len(rest) == 1:
        (xo_ref,) = rest
    else:
        nw_ref, xo_ref, ho_ref = rest
        hn = xn * lax.rsqrt(jnp.mean(xn * xn, axis=-1, keepdims=True) + EPS) * nw_ref[...]
        ho_ref[...] = hn.astype(ho_ref.dtype)
    xo_ref[...] = xn


def _out_proj(merged, w_out, l, x, post_w, next_pre_w=None):
    n, d = x.shape
    tm = min(512, n)
    tok = lambda: pl.BlockSpec((tm, d), lambda i: (i, 0))
    vec = lambda: pl.BlockSpec((1, d), lambda i: (0, 0))
    in_specs = [tok(), pl.BlockSpec((None, d, d), lambda i: (l, 0, 0)), tok(), vec()]
    args = [merged, w_out, x, post_w.reshape(1, d)]
    out_shape = [jax.ShapeDtypeStruct((n, d), F32)]
    out_specs = [tok()]
    if next_pre_w is not None:
        in_specs.append(vec())
        args.append(next_pre_w.reshape(1, d))
        out_shape.append(jax.ShapeDtypeStruct((n, d), BF16))
        out_specs.append(tok())
    return pl.pallas_call(
        _out_kernel,
        out_shape=tuple(out_shape),
        grid=(n // tm,),
        in_specs=in_specs,
        out_specs=tuple(out_specs),
        compiler_params=_cparams(("parallel",), VMEM_LIMIT),
        name="out_proj",
    )(*args)


def _permute_w_in(w_in):
    depth, d, _ = w_in.shape
    w_in = w_in.astype(BF16)
    na = w_in[:, :, :NA_COLS]
    diff = w_in[:, :, NA_COLS:NA_COLS + DIFF_COLS]
    g0 = NA_COLS + DIFF_COLS
    gla_qkv = w_in[:, :, g0:g0 + GLA_QKV_COLS]
    gl = w_in[:, :, g0 + GLA_QKV_COLS:g0 + GLA_COLS]
    m0 = g0 + GLA_COLS
    cq_ckv = w_in[:, :, m0:m0 + MLA_Q_LORA + MLA_KV_LORA]
    kr = w_in[:, :, m0 + MLA_Q_LORA + MLA_KV_LORA:m0 + MLA_COLS]
    z = w_in[:, :, m0 + MLA_COLS:]
    used = COL_KRGL + MLA_ROPE + 2 * GLA_RANK
    pad = jnp.zeros((depth, d, PROJ_COLS - used), w_in.dtype)
    return jnp.concatenate([na, diff, gla_qkv, cq_ckv, z, kr, gl, pad], axis=-1).astype(BF16)


def _pad_w_qb(w_qb):
    depth, r, _ = w_qb.shape
    w = w_qb.reshape(depth, r, N_HEADS, MLA_NOPE + MLA_ROPE)
    w = jnp.pad(w, ((0, 0), (0, 0), (0, 0), (0, VAUG_W - MLA_NOPE - MLA_ROPE)))
    return w.reshape(depth, r, N_HEADS * VAUG_W).astype(BF16)


def _rope_tables(s):
    half = DIFF_D // 2
    inv = ROPE_THETA ** (-jnp.arange(0, DIFF_D, 2, dtype=F32) / DIFF_D)
    ang = jnp.arange(s, dtype=F32)[:, None] * inv[None, :]
    cos, sin = jnp.cos(ang), jnp.sin(ang)
    zero = jnp.zeros_like(sin)
    reps = LANES // DIFF_D
    c = jnp.tile(jnp.concatenate([cos, cos], axis=1), (1, reps))
    s1 = jnp.tile(jnp.concatenate([-sin, zero], axis=1), (1, reps))
    s2 = jnp.tile(jnp.concatenate([zero, sin], axis=1), (1, reps))
    assert half * 2 * reps == LANES
    return c, s1, s2


def _trunk(x, p):
    b, s, d = x.shape
    n = b * s
    tables = _rope_tables(s)
    xf = x.reshape(n, d)
    h = _rmsnorm(xf, p["pre_norm_w"][0])
    for l in range(DEPTH):
        lam_init = 0.8 - 0.6 * math.exp(-0.3 * l)
        proj = _in_proj(h, p["w_in"], l)
        proj3 = proj.reshape(b, s, PROJ_COLS)
        dq, dk, dva, mq, mk, mva = _prep(proj3, tables, p["mla_q_norm_w"][l], p["mla_w_qb"][l],
                                         p["mla_kv_norm_w"][l], p["mla_w_kvb"][l])
        o_a = _na(proj3, p["na_bias"], l)
        o_b = _flash(dq, dk, dva, n_maps=2, lam_qk=p["diff_lambda_qk"][l],
                     subw=p["diff_subln_w"][l], lam_init=lam_init)
        o_c = _gla(proj3, p["gla_w_gate_up"][l], p["gla_b_gate"][l], p["gla_norm_w"][l])
        o_d = _flash(mq, mk, mva, n_maps=1)
        branches = [o.reshape(n, BRANCH_W) for o in (o_a, o_b, o_c, o_d)]
        merged = _merge(h, branches, proj, p["w_gate"], p["w_branch"], l)
        if l + 1 < DEPTH:
            xf, h = _out_proj(merged, p["w_out"], l, xf, p["post_norm_w"][l], p["pre_norm_w"][l + 1])
        else:
            (xf,) = _out_proj(merged, p["w_out"], l, xf, p["post_norm_w"][l])
    return xf.reshape(b, s, d)


def kernel(x_prompt, x_sample, pre_norm_w, w_in, na_rpb, diff_lambda_qk, diff_subln_w, gla_w_gate_up, gla_b_gate, gla_norm_w, mla_q_norm_w, mla_w_qb, mla_kv_norm_w, mla_w_kvb, w_gate, w_branch, w_out, post_norm_w):
    p = {
        "pre_norm_w": pre_norm_w,
        "w_in": _permute_w_in(w_in),
        "na_bias": _na_bias_tiles(na_rpb),
        "diff_lambda_qk": diff_lambda_qk,
        "diff_subln_w": diff_subln_w,
        "gla_w_gate_up": gla_w_gate_up,
        "gla_b_gate": gla_b_gate,
        "gla_norm_w": gla_norm_w,
        "mla_q_norm_w": mla_q_norm_w,
        "mla_w_qb": _pad_w_qb(mla_w_qb),
        "mla_kv_norm_w": mla_kv_norm_w,
        "mla_w_kvb": mla_w_kvb.astype(BF16),
        "w_gate": w_gate.astype(BF16),
        "w_branch": w_branch.astype(BF16),
        "w_out": w_out.astype(BF16),
        "post_norm_w": post_norm_w,
    }
    return (_trunk(x_prompt, p), _trunk(x_sample, p))
```

```python
import functools
import math

import jax
import jax.numpy as jnp
import numpy as np
from jax import lax
from jax.experimental import pallas as pl
from jax.experimental.pallas import tpu as pltpu

F32 = jnp.float32
BF16 = jnp.bfloat16

DEPTH = 4
GRID_W = 64
N_BRANCH = 4
BRANCH_W = 512
N_HEADS = 4
NA_HEAD_DIM = 128
WIN_R = 8
WIN_C = 16
DIFF_D = 64
GLA_DK = 64
GLA_DV = 128
GLA_RANK = 16
GLA_TAU = 16.0
GLA_CHUNK = 64
MLA_Q_LORA = 384
MLA_KV_LORA = 128
MLA_NOPE = 128
MLA_ROPE = 64
MLA_V = 128
ROPE_THETA = 10000.0
EPS = 1e-6
LOG2E = math.log2(math.e)

NA_COLS = 3 * N_HEADS * NA_HEAD_DIM
DIFF_COLS = 3 * N_HEADS * 2 * DIFF_D
GLA_QKV_COLS = 2 * N_HEADS * GLA_DK + N_HEADS * GLA_DV
GLA_COLS = GLA_QKV_COLS + 2 * GLA_RANK
MLA_COLS = MLA_Q_LORA + MLA_KV_LORA + MLA_ROPE
Z_COLS = N_BRANCH * BRANCH_W

LANES = 128
V7X_VMEM_BYTES = 64 * 1024 * 1024
VMEM_LIMIT = 56 * 1024 * 1024
MIB = 1024 * 1024

COL_NA = 0
COL_DIFF = COL_NA + NA_COLS
COL_GLA = COL_DIFF + DIFF_COLS
COL_MLA = COL_GLA + GLA_QKV_COLS
COL_Z = COL_MLA + MLA_Q_LORA + MLA_KV_LORA
COL_KRGL = COL_Z + Z_COLS
PROJ_TN = 1152
PROJ_COLS = 6 * PROJ_TN
GL_LANE0 = MLA_ROPE

NA_QROWS = 8
NA_KROWS = 16
NA_TQ = NA_QROWS * GRID_W
NA_TK = NA_KROWS * GRID_W
NA_KV_PARTS = 4
NEG_BIG = -1e30

VAUG_W = 2 * LANES
FLASH_TQ = 1024
FLASH_TK = 1024
FLASH_UNROLL = 2
MERGE_TN = 512
GLA_T = 256
GLA_ROWS_PER_STEP = 2


def _cparams(sem, vmem=None):
    return pltpu.CompilerParams(dimension_semantics=sem, vmem_limit_bytes=vmem)


def _nt_dot(a, b):
    return lax.dot_general(a, b, (((1,), (1,)), ((), ())), preferred_element_type=F32)


def _tn_dot(a, b):
    return lax.dot_general(a, b, (((0,), (0,)), ((), ())), preferred_element_type=F32)


def _rmsnorm_kernel(x_ref, w_ref, o_ref):
    x = x_ref[...]
    y = x * lax.rsqrt(jnp.mean(x * x, axis=-1, keepdims=True) + EPS)
    o_ref[...] = (y * w_ref[...]).astype(o_ref.dtype)


def _rmsnorm(x, w):
    n, d = x.shape
    tm = min(512, n)
    return pl.pallas_call(
        _rmsnorm_kernel,
        out_shape=jax.ShapeDtypeStruct((n, d), BF16),
        grid=(n // tm,),
        in_specs=[pl.BlockSpec((tm, d), lambda i: (i, 0)),
                  pl.BlockSpec((1, d), lambda i: (0, 0))],
        out_specs=pl.BlockSpec((tm, d), lambda i: (i, 0)),
        compiler_params=_cparams(("parallel",)),
        name="pre_norm",
    )(x, w.reshape(1, d))


def _matmul_kernel(a_ref, b_ref, o_ref):
    o_ref[...] = jnp.dot(a_ref[...], b_ref[...], preferred_element_type=F32).astype(o_ref.dtype)


def _in_proj(h, w, l):
    n, d = h.shape
    c = w.shape[2]
    tm = min(1024, n)
    tn = PROJ_TN
    return pl.pallas_call(
        _matmul_kernel,
        out_shape=jax.ShapeDtypeStruct((n, c), BF16),
        grid=(n // tm, c // tn),
        in_specs=[pl.BlockSpec((tm, d), lambda i, j: (i, 0)),
                  pl.BlockSpec((None, d, tn), lambda i, j: (l, 0, j))],
        out_specs=pl.BlockSpec((tm, tn), lambda i, j: (i, j)),
        compiler_params=_cparams(("parallel", "parallel"), VMEM_LIMIT),
        name="in_proj",
    )(h, w)


def _rope_lanes(x, c, s1, s2):
    return x * c + pltpu.roll(x, LANES - 32, 1) * s1 + pltpu.roll(x, 32, 1) * s2


def _prep_kernel(dq_ref, dk_ref, dv_ref, cqkv_ref, krgl_ref, cos_ref, s1_ref, s2_ref,
                 qnw_ref, wqb_ref, kvnw_ref, wkvb_ref,
                 dqo_ref, dko_ref, dvo_ref, mqo_ref, mko_ref, mvo_ref):
    c, s1, s2 = cos_ref[...], s1_ref[...], s2_ref[...]
    t = c.shape[0]
    lane = lax.broadcasted_iota(jnp.int32, (t, LANES), 1)
    ones_col = jnp.where(lane == 0, 1.0, 0.0).astype(BF16)
    diff_qscale = (DIFF_D ** -0.5) * LOG2E
    mla_qscale = ((MLA_NOPE + MLA_ROPE) ** -0.5) * LOG2E

    for hd in range(N_HEADS):
        sl = slice(hd * LANES, (hd + 1) * LANES)
        q = _rope_lanes(dq_ref[0, :, sl].astype(F32), c, s1, s2) * diff_qscale
        dqo_ref[0, :, sl] = q.astype(BF16)
        k = _rope_lanes(dk_ref[0, :, sl].astype(F32), c, s1, s2)
        dko_ref[0, :, sl] = k.astype(BF16)
        dvo_ref[0, :, hd * VAUG_W:hd * VAUG_W + LANES] = dv_ref[0, :, sl]
        dvo_ref[0, :, hd * VAUG_W + LANES:(hd + 1) * VAUG_W] = ones_col

    cq = cqkv_ref[0, :, :MLA_Q_LORA].astype(F32)
    cqn = cq * lax.rsqrt(jnp.mean(cq * cq, axis=-1, keepdims=True) + EPS) * qnw_ref[...]
    qm = jnp.dot(cqn.astype(BF16), wqb_ref[...], preferred_element_type=F32)
    ckv = cqkv_ref[0, :, MLA_Q_LORA:].astype(F32)
    ckvn = ckv * lax.rsqrt(jnp.mean(ckv * ckv, axis=-1, keepdims=True) + EPS) * kvnw_ref[...]
    kv = jnp.dot(ckvn.astype(BF16), wkvb_ref[...], preferred_element_type=F32)
    kr = jnp.where(lane < MLA_ROPE, krgl_ref[0].astype(F32), 0.0)
    kr = _rope_lanes(kr, c, s1, s2).astype(BF16)
    for hd in range(N_HEADS):
        lo, mid, hi = hd * VAUG_W, hd * VAUG_W + LANES, (hd + 1) * VAUG_W
        mqo_ref[0, :, lo:mid] = (qm[:, lo:mid] * mla_qscale).astype(BF16)
        qr = _rope_lanes(qm[:, mid:hi], c, s1, s2) * mla_qscale
        mqo_ref[0, :, mid:hi] = qr.astype(BF16)
        mko_ref[0, :, lo:mid] = kv[:, lo:mid].astype(BF16)
        mko_ref[0, :, mid:hi] = kr
        mvo_ref[0, :, lo:mid] = kv[:, mid:hi].astype(BF16)
        mvo_ref[0, :, mid:hi] = ones_col


def _prep(proj3, tables, qnw, wqb, kvnw, wkvb):
    b, s, _ = proj3.shape
    tm = min(512, s)
    cos, s1, s2 = tables
    hw = N_HEADS * LANES
    aw = N_HEADS * VAUG_W
    tok = lambda cb: (lambda bi, i: (bi, i, cb))
    tab = pl.BlockSpec((tm, LANES), lambda bi, i: (i, 0))
    const = lambda shape: pl.BlockSpec(shape, lambda bi, i: (0, 0))
    out_sd = lambda w: jax.ShapeDtypeStruct((b, s, w), BF16)
    out_bs = lambda w: pl.BlockSpec((1, tm, w), lambda bi, i: (bi, i, 0))
    return pl.pallas_call(
        _prep_kernel,
        out_shape=(out_sd(hw), out_sd(hw), out_sd(aw), out_sd(aw), out_sd(aw), out_sd(aw)),
        grid=(b, s // tm),
        in_specs=[pl.BlockSpec((1, tm, hw), tok(COL_DIFF // hw)),
                  pl.BlockSpec((1, tm, hw), tok(COL_DIFF // hw + 1)),
                  pl.BlockSpec((1, tm, hw), tok(COL_DIFF // hw + 2)),
                  pl.BlockSpec((1, tm, hw), tok(COL_MLA // hw)),
                  pl.BlockSpec((1, tm, LANES), tok(COL_KRGL // LANES)),
                  tab, tab, tab,
                  const((1, MLA_Q_LORA)), const(wqb.shape),
                  const((1, MLA_KV_LORA)), const(wkvb.shape)],
        out_specs=(out_bs(hw), out_bs(hw), out_bs(aw), out_bs(aw), out_bs(aw), out_bs(aw)),
        compiler_params=_cparams(("parallel", "parallel"), VMEM_LIMIT),
        name="branch_prep",
    )(proj3, proj3, proj3, proj3, proj3, cos, s1, s2,
      qnw.reshape(1, -1), wqb, kvnw.reshape(1, -1), wkvb)


def _flash_kernel(*refs, n_maps, tk, lam_init):
    if n_maps == 2:
        q_ref, kt_ref, v_ref, lam_ref, subw_ref, o_ref, m_scr, acc_scr, s_scr = refs
    else:
        q_ref, kt_ref, v_ref, o_ref, m_scr, acc_scr, s_scr = refs
    nk = kt_ref.shape[2] // tk
    assert nk >= 2 and nk % 2 == 0
    q = q_ref[0]
    if n_maps == 2:
        lane = lax.broadcasted_iota(jnp.int32, q.shape, 1)
        zero = jnp.zeros_like(q)
        qs = (jnp.where(lane < DIFF_D, q, zero), jnp.where(lane >= DIFF_D, q, zero))
    else:
        qs = (q,)

    m_scr[...] = jnp.full(m_scr.shape, -jnp.inf, F32)
    acc_scr[...] = jnp.zeros(acc_scr.shape, F32)

    def scores(j, slot):
        kt = kt_ref[0, :, pl.ds(pl.multiple_of(j * tk, tk), tk)]
        for mi in range(n_maps):
            s_scr[slot, mi] = jnp.dot(qs[mi], kt, preferred_element_type=F32)

    def accumulate(j, slot):
        v = v_ref[0, pl.ds(pl.multiple_of(j * tk, tk), tk), :]
        for mi in range(n_maps):
            s = s_scr[slot, mi]
            m_prev = m_scr[mi]
            m_new = jnp.maximum(m_prev, jnp.max(s, axis=1, keepdims=True))
            alpha = jnp.exp2(m_prev - m_new)
            p = jnp.exp2(s - jnp.tile(m_new, (1, tk // LANES)))
            pv = jnp.dot(p.astype(BF16), v, preferred_element_type=F32)
            acc_scr[mi] = acc_scr[mi] * jnp.tile(alpha, (1, VAUG_W // LANES)) + pv
            m_scr[mi] = m_new

    unroll = min(FLASH_UNROLL, nk)
    assert nk % unroll == 0 and unroll % 2 == 0
    scores(0, 0)

    def body(i, carry):
        j0 = unroll * i
        for t in range(unroll):
            scores(j0 + t + 1, (t + 1) % 2)
            accumulate(j0 + t, t % 2)
        return carry

    lax.fori_loop(0, nk // unroll - 1, body, 0)
    j0 = nk - unroll
    for t in range(unroll):
        if t + 1 < unroll:
            scores(j0 + t + 1, (t + 1) % 2)
        accumulate(j0 + t, t % 2)

    outs = []
    for mi in range(n_maps):
        acc = acc_scr[mi]
        outs.append(acc[:, :LANES] / acc[:, LANES:LANES + 1])
    if n_maps == 2:
        lq = lam_ref[...]
        lam = (jnp.exp(jnp.sum(lq[0:1] * lq[1:2], axis=1, keepdims=True))
               - jnp.exp(jnp.sum(lq[2:3] * lq[3:4], axis=1, keepdims=True)) + lam_init)
        o = outs[0] - lam * outs[1]
        y = o * lax.rsqrt(jnp.mean(o * o, axis=-1, keepdims=True) + EPS) * subw_ref[...]
        o = y * (1.0 - lam_init)
    else:
        o = outs[0]
    o_ref[0] = o.astype(o_ref.dtype)


def _flash(q, k, vaug, *, n_maps, lam_qk=None, subw=None, lam_init=0.0):
    b, s, qw_all = q.shape
    qw = qw_all // N_HEADS
    kw = k.shape[2] // N_HEADS
    kt = jnp.swapaxes(k, 1, 2)
    tq = min(FLASH_TQ, s)
    tk = min(FLASH_TK, s // 2)
    in_specs = [pl.BlockSpec((1, tq, qw), lambda bi, hd, i: (bi, i, hd)),
                pl.BlockSpec((1, kw, s), lambda bi, hd, i: (bi, hd, 0)),
                pl.BlockSpec((1, s, VAUG_W), lambda bi, hd, i: (bi, 0, hd))]
    args = [q, kt, vaug]
    if n_maps == 2:
        in_specs += [pl.BlockSpec(lam_qk.shape, lambda bi, hd, i: (0, 0)),
                     pl.BlockSpec((1, LANES), lambda bi, hd, i: (0, 0))]
        args += [lam_qk, subw.reshape(1, LANES)]
    return pl.pallas_call(
        functools.partial(_flash_kernel, n_maps=n_maps, tk=tk, lam_init=lam_init),
        out_shape=jax.ShapeDtypeStruct((b, s, N_HEADS * LANES), BF16),
        grid=(b, N_HEADS, s // tq),
        in_specs=in_specs,
        out_specs=pl.BlockSpec((1, tq, LANES), lambda bi, hd, i: (bi, i, hd)),
        scratch_shapes=[pltpu.VMEM((n_maps, tq, LANES), F32),
                        pltpu.VMEM((n_maps, tq, VAUG_W), F32),
                        pltpu.VMEM((2, n_maps, tq, tk), F32)],
        compiler_params=_cparams(("parallel", "parallel", "parallel"), VMEM_LIMIT),
        name="diff_attention" if n_maps == 2 else "latent_attention",
    )(*args)


def _na_kernel(q_ref, *refs):
    k_refs, v_refs = refs[:NA_KV_PARTS], refs[NA_KV_PARTS:2 * NA_KV_PARTS]
    b_ref, o_ref = refs[2 * NA_KV_PARTS:]
    for hd in range(N_HEADS):
        sl = slice(hd * LANES, (hd + 1) * LANES)
        k = jnp.concatenate([r[0, :, sl] for r in k_refs], axis=0)
        v = jnp.concatenate([r[0, :, sl] for r in v_refs], axis=0)
        s = _nt_dot(q_ref[0, :, sl], k) * (NA_HEAD_DIM ** -0.5) + b_ref[0, hd].astype(F32)
        m = jnp.max(s, axis=1, keepdims=True)
        p = jnp.exp(s - m)
        l = jnp.sum(p, axis=1, keepdims=True)
        o = jnp.dot(p.astype(BF16), v, preferred_element_type=F32) / l
        o_ref[0, :, sl] = o.astype(o_ref.dtype)


def _na_tile_kernel(toep_ref, o_ref):
    off = pl.program_id(1) * (WIN_R // 2)
    for a in range(NA_QROWS):
        lo = jnp.clip(a + off - WIN_R // 2, 0, NA_KROWS - WIN_R)
        pieces = []
        for w in range(NA_KROWS):
            d = w - a - off + (WIN_R - 1)
            piece = toep_ref[jnp.clip(d, 0, 2 * WIN_R - 2)]
            pieces.append(jnp.where((w >= lo) & (w < lo + WIN_R), piece, NEG_BIG))
        o_ref[a * GRID_W:(a + 1) * GRID_W, :] = jnp.concatenate(pieces, axis=1).astype(o_ref.dtype)


def _na_bias_tiles(rpb):
    c = np.arange(GRID_W)
    cs = np.clip(c - WIN_C // 2, 0, GRID_W - WIN_C)
    col_ok = (c[None, :] >= cs[:, None]) & (c[None, :] < cs[:, None] + WIN_C)
    dc = c[None, :] - c[:, None] + (WIN_C - 1)
    sel_c = ((dc[..., None] == np.arange(2 * WIN_C - 1)) & col_ok[..., None]).astype(np.float32)
    toep = jnp.einsum("qkc,lhdc->lhdqk", sel_c, rpb.astype(F32), precision=lax.Precision.HIGHEST)
    toep = jnp.where(col_ok, toep, NEG_BIG)
    depth = rpb.shape[0]
    return pl.pallas_call(
        _na_tile_kernel,
        out_shape=jax.ShapeDtypeStruct((depth, 3, N_HEADS, NA_TQ, NA_TK), BF16),
        grid=(depth, 3, N_HEADS),
        in_specs=[pl.BlockSpec((None, None, 2 * WIN_R - 1, GRID_W, GRID_W), lambda l, t, h: (l, h, 0, 0, 0))],
        out_specs=pl.BlockSpec((None, None, None, NA_TQ, NA_TK), lambda l, t, h: (l, t, h, 0, 0)),
        compiler_params=_cparams(("parallel", "parallel", "parallel")),
        name="na_bias_tiles",
    )(toep)


def _na(proj3, bias_tiles, l):
    b, s, _ = proj3.shape
    rows = s // GRID_W
    assert rows % NA_QROWS == 0 and rows >= NA_KROWS
    nq = rows // NA_QROWS
    hw = N_HEADS * LANES
    part_rows = NA_KROWS // NA_KV_PARTS
    part = part_rows * GRID_W
    qb = COL_NA // hw

    def kv_spec(cb, t):
        def idx(i, bi):
            ks_row = jnp.clip(i * NA_QROWS - WIN_R // 2, 0, rows - NA_KROWS)
            return (bi, ks_row // part_rows + t, cb)
        return pl.BlockSpec((1, part, hw), idx)

    def bias_idx(i, bi):
        kind = jnp.where(i == 0, 0, jnp.where(i == nq - 1, 2, 1))
        return (l, kind, 0, 0, 0)

    return pl.pallas_call(
        _na_kernel,
        out_shape=jax.ShapeDtypeStruct((b, s, hw), BF16),
        grid=(nq, b),
        in_specs=([pl.BlockSpec((1, NA_TQ, hw), lambda i, bi: (bi, i, qb))]
                  + [kv_spec(qb + 1, t) for t in range(NA_KV_PARTS)]
                  + [kv_spec(qb + 2, t) for t in range(NA_KV_PARTS)]
                  + [pl.BlockSpec((None, 1, N_HEADS, NA_TQ, NA_TK), bias_idx)]),
        out_specs=pl.BlockSpec((1, NA_TQ, hw), lambda i, bi: (bi, i, 0)),
        compiler_params=_cparams(("parallel", "parallel"), VMEM_LIMIT),
        name="neighborhood_attention",
    )(proj3, *([proj3] * (2 * NA_KV_PARTS)), bias_tiles)


def _gla_kernel(*refs, reverse):
    if reverse:
        q_ref, k_ref, v_ref, gl_ref, wg_ref, bg_ref, ofwd_ref, nw_ref, o_ref, st_scr = refs
    else:
        q_ref, k_ref, v_ref, gl_ref, wg_ref, bg_ref, o_ref, st_scr = refs
    rows_per_step, t = q_ref.shape[0], q_ref.shape[1]
    gw = N_HEADS * GLA_DK
    c_len = GLA_CHUNK

    @pl.when(pl.program_id(1) == 0)
    def _():
        st_scr[...] = jnp.zeros(st_scr.shape, F32)

    pos = lax.broadcasted_iota(jnp.int32, (t, gw), 0) & (c_len - 1)
    ref_i = c_len // 2 if reverse else c_len // 2 - 1
    last_i = 0 if reverse else c_len - 1
    n_chunks = t // c_len
    order = range(n_chunks - 1, -1, -1) if reverse else range(n_chunks)
    ri = lax.broadcasted_iota(jnp.int32, (t, t), 0)
    ci = lax.broadcasted_iota(jnp.int32, (t, t), 1)
    same_chunk = (ri // c_len) == (ci // c_len)
    intra = same_chunk & ((ci >= ri) if reverse else (ri >= ci))
    lane = lax.broadcasted_iota(jnp.int32, (t, LANES), 1)
    row_chunk = lax.broadcasted_iota(jnp.int32, (t, LANES), 0) // c_len

    def spread(x):
        return jnp.concatenate([jnp.where(row_chunk == n, x, 0.0) for n in range(n_chunks)], axis=1)

    def per_chunk_rows(x, i):
        return jnp.concatenate([jnp.broadcast_to(x[n * c_len + i:n * c_len + i + 1], (c_len, x.shape[1]))
                                for n in range(n_chunks)], axis=0)

    for bi in range(rows_per_step):
        gpre = jnp.dot(gl_ref[bi], wg_ref[...], preferred_element_type=F32) + bg_ref[...]
        g = (jnp.minimum(gpre, 0.0) - jnp.log1p(jnp.exp(-jnp.abs(gpre)))) * (1.0 / GLA_TAU)

        cum = g
        for sh in (1, 2, 4, 8, 16, 32):
            if reverse:
                cum = cum + jnp.where(pos < c_len - sh, pltpu.roll(cum, t - sh, 0), 0.0)
            else:
                cum = cum + jnp.where(pos >= sh, pltpu.roll(cum, sh, 0), 0.0)

        ref = per_chunk_rows(cum, ref_i)
        last = per_chunk_rows(cum, last_i)
        qf = q_ref[bi].astype(F32) * (GLA_DK ** -0.5)
        kf = k_ref[bi].astype(F32)
        qd = qf * jnp.exp(cum - ref)
        kd = (kf * jnp.exp(ref - cum)).astype(BF16)
        qe = qf * jnp.exp(cum)
        kl = kf * jnp.exp(last - cum)
        dec = jnp.exp(last)

        for pr in range(N_HEADS // 2):
            lsl = slice(pr * LANES, (pr + 1) * LANES)
            owns = (lane < GLA_DK, lane >= GLA_DK)
            qd_p, qe_p = qd[:, lsl], qe[:, lsl]
            lhs = jnp.concatenate([jnp.where(own, qd_p, 0.0) for own in owns], axis=0).astype(BF16)
            a_pair = _nt_dot(lhs, kd[:, lsl])
            kl_spread = spread(kl[:, lsl]).astype(BF16)
            for hh in range(2):
                hd = 2 * pr + hh
                vsl = slice(hd * GLA_DV, (hd + 1) * GLA_DV)
                v_h = v_ref[bi, :, vsl]
                a = jnp.where(intra, a_pair[hh * t:(hh + 1) * t], 0.0)
                o_h = jnp.dot(a.astype(BF16), v_h, preferred_element_type=F32)
                ds_all = _tn_dot(v_h, kl_spread)
                st = st_scr[bi, hd]
                before = [None] * n_chunks
                for n in order:
                    before[n] = st
                    dec_n = dec[n * c_len:n * c_len + 1, lsl]
                    st = st * dec_n + ds_all[:, n * LANES:(n + 1) * LANES]
                st_scr[bi, hd] = st
                st_cat = jnp.concatenate(before, axis=1).astype(BF16)
                qe_spread = spread(jnp.where(owns[hh], qe_p, 0.0)).astype(BF16)
                o_h = o_h + _nt_dot(qe_spread, st_cat)
                if reverse:
                    tot = o_h + ofwd_ref[bi, :, vsl]
                    y = tot * lax.rsqrt(jnp.mean(tot * tot, axis=-1, keepdims=True) + EPS) * nw_ref[...]
                    o_ref[bi, :, vsl] = y.astype(o_ref.dtype)
                else:
                    o_ref[bi, :, vsl] = o_h


def _gla_pass(proj3, wg, bg, reverse, o_fwd=None, norm_w=None):
    b, s, _ = proj3.shape
    t = min(GLA_T, s)
    nb = s // t
    gw = N_HEADS * GLA_DK
    vw = N_HEADS * GLA_DV
    blk = (lambda i: nb - 1 - i) if reverse else (lambda i: i)
    tok = lambda cb: (lambda bi, i: (bi, blk(i), cb))
    const = lambda shape: pl.BlockSpec(shape, lambda bi, i: (0, 0))
    rb = GLA_ROWS_PER_STEP
    assert b % rb == 0
    in_specs = [pl.BlockSpec((rb, t, gw), tok(COL_GLA // gw)),
                pl.BlockSpec((rb, t, gw), tok(COL_GLA // gw + 1)),
                pl.BlockSpec((rb, t, vw), tok((COL_GLA + 2 * gw) // vw)),
                pl.BlockSpec((rb, t, LANES), tok(COL_KRGL // LANES)),
                const((LANES, gw)), const((1, gw))]
    args = [proj3, proj3, proj3, proj3, wg, bg]
    if reverse:
        in_specs += [pl.BlockSpec((rb, t, vw), tok(0)), const((1, GLA_DV))]
        args += [o_fwd, norm_w.reshape(1, GLA_DV)]
    return pl.pallas_call(
        functools.partial(_gla_kernel, reverse=reverse),
        out_shape=jax.ShapeDtypeStruct((b, s, vw), BF16 if reverse else F32),
        grid=(b // rb, nb),
        in_specs=in_specs,
        out_specs=pl.BlockSpec((rb, t, vw), tok(0)),
        scratch_shapes=[pltpu.VMEM((rb, N_HEADS, GLA_DV, LANES), F32)],
        compiler_params=_cparams(("parallel", "arbitrary")),
        name="gla_backward" if reverse else "gla_forward",
    )(*args)


def _gla(proj3, w_gate_up, b_gate, norm_w):
    gw = N_HEADS * GLA_DK
    wgs = []
    for d in range(2):
        lo = GL_LANE0 + d * GLA_RANK
        wgs.append(jnp.zeros((LANES, gw), F32).at[lo:lo + GLA_RANK].set(w_gate_up[d]).astype(BF16))
    o_fwd = _gla_pass(proj3, wgs[0], b_gate[0].reshape(1, gw), False)
    return _gla_pass(proj3, wgs[1], b_gate[1].reshape(1, gw), True, o_fwd, norm_w)


def _merge_kernel(h_ref, oa_ref, ob_ref, oc_ref, od_ref, za_ref, zb_ref, zc_ref, zd_ref,
                  wg_ref, wb_ref, o_ref):
    h = h_ref[...]
    acc = None
    for i, (o_r, z_r) in enumerate(((oa_ref, za_ref), (ob_ref, zb_ref), (oc_ref, zc_ref), (od_ref, zd_ref))):
        z = z_r[...].astype(F32)
        u = (o_r[...].astype(F32) * (z * jax.nn.sigmoid(z))).astype(BF16)
        t = jnp.dot(u, wb_ref[i], preferred_element_type=F32)
        gate = jax.nn.sigmoid(jnp.dot(h, wg_ref[i], preferred_element_type=F32))
        acc = gate * t if acc is None else acc + gate * t
    o_ref[...] = acc.astype(o_ref.dtype)


def _merge(h, branches, proj, wg, wb, l):
    n, d = h.shape
    tm = min(512, n)
    tn = min(MERGE_TN, d)
    zb0 = COL_Z // BRANCH_W
    tok = pl.BlockSpec((tm, BRANCH_W), lambda j, i: (i, 0))
    zspec = lambda k: pl.BlockSpec((tm, BRANCH_W), lambda j, i: (i, zb0 + k))
    return pl.pallas_call(
        _merge_kernel,
        out_shape=jax.ShapeDtypeStruct((n, d), BF16),
        grid=(d // tn, n // tm),
        in_specs=[pl.BlockSpec((tm, d), lambda j, i: (i, 0)), tok, tok, tok, tok,
                  zspec(0), zspec(1), zspec(2), zspec(3),
                  pl.BlockSpec((None, N_BRANCH, d, tn), lambda j, i: (l, 0, 0, j)),
                  pl.BlockSpec((None, N_BRANCH, BRANCH_W, tn), lambda j, i: (l, 0, 0, j))],
        out_specs=pl.BlockSpec((tm, tn), lambda j, i: (i, j)),
        compiler_params=_cparams(("parallel", "parallel"), VMEM_LIMIT),
        name="branch_merge",
    )(h, *branches, proj, proj, proj, proj, wg, wb)


def _out_kernel(m_ref, w_ref, x_ref, pw_ref, *rest):
    y = jnp.dot(m_ref[...], w_ref[...], preferred_element_type=F32)
    y = y * lax.rsqrt(jnp.mean(y * y, axis=-1, keepdims=True) + EPS) * pw_ref[...]
    xn = x_ref[...] + y
    if len(rest) == 1:
        (xo_ref,) = rest
    else:
        nw_ref, xo_ref, ho_ref = rest
        hn = xn * lax.rsqrt(jnp.mean(xn * xn, axis=-1, keepdims=True) + EPS) * nw_ref[...]
        ho_ref[...] = hn.astype(ho_ref.dtype)
    xo_ref[...] = xn


def _out_proj(merged, w_out, l, x, post_w, next_pre_w=None):
    n, d = x.shape
    tm = min(512, n)
    tok = lambda: pl.BlockSpec((tm, d), lambda i: (i, 0))
    vec = lambda: pl.BlockSpec((1, d), lambda i: (0, 0))
    in_specs = [tok(), pl.BlockSpec((None, d, d), lambda i: (l, 0, 0)), tok(), vec()]
    args = [merged, w_out, x, post_w.reshape(1, d)]
    out_shape = [jax.ShapeDtypeStruct((n, d), F32)]
    out_specs = [tok()]
    if next_pre_w is not None:
        in_specs.append(vec())
        args.append(next_pre_w.reshape(1, d))
        out_shape.append(jax.ShapeDtypeStruct((n, d), BF16))
        out_specs.append(tok())
    return pl.pallas_call(
        _out_kernel,
        out_shape=tuple(out_shape),
        grid=(n // tm,),
        in_specs=in_specs,
        out_specs=tuple(out_specs),
        compiler_params=_cparams(("parallel",), VMEM_LIMIT),
        name="out_proj",
    )(*args)


def _permute_w_in(w_in):
    depth, d, _ = w_in.shape
    w_in = w_in.astype(BF16)
    na = w_in[:, :, :NA_COLS]
    diff = w_in[:, :, NA_COLS:NA_COLS + DIFF_COLS]
    g0 = NA_COLS + DIFF_COLS
    gla_qkv = w_in[:, :, g0:g0 + GLA_QKV_COLS]
    gl = w_in[:, :, g0 + GLA_QKV_COLS:g0 + GLA_COLS]
    m0 = g0 + GLA_COLS
    cq_ckv = w_in[:, :, m0:m0 + MLA_Q_LORA + MLA_KV_LORA]
    kr = w_in[:, :, m0 + MLA_Q_LORA + MLA_KV_LORA:m0 + MLA_COLS]
    z = w_in[:, :, m0 + MLA_COLS:]
    used = COL_KRGL + MLA_ROPE + 2 * GLA_RANK
    pad = jnp.zeros((depth, d, PROJ_COLS - used), w_in.dtype)
    return jnp.concatenate([na, diff, gla_qkv, cq_ckv, z, kr, gl, pad], axis=-1).astype(BF16)


def _pad_w_qb(w_qb):
    depth, r, _ = w_qb.shape
    w = w_qb.reshape(depth, r, N_HEADS, MLA_NOPE + MLA_ROPE)
    w = jnp.pad(w, ((0, 0), (0, 0), (0, 0), (0, VAUG_W - MLA_NOPE - MLA_ROPE)))
    return w.reshape(depth, r, N_HEADS * VAUG_W).astype(BF16)


def _rope_tables(s):
    half = DIFF_D // 2
    inv = ROPE_THETA ** (-jnp.arange(0, DIFF_D, 2, dtype=F32) / DIFF_D)
    ang = jnp.arange(s, dtype=F32)[:, None] * inv[None, :]
    cos, sin = jnp.cos(ang), jnp.sin(ang)
    zero = jnp.zeros_like(sin)
    reps = LANES // DIFF_D
    c = jnp.tile(jnp.concatenate([cos, cos], axis=1), (1, reps))
    s1 = jnp.tile(jnp.concatenate([-sin, zero], axis=1), (1, reps))
    s2 = jnp.tile(jnp.concatenate([zero, sin], axis=1), (1, reps))
    assert half * 2 * reps == LANES
    return c, s1, s2


def _trunk(x, p):
    b, s, d = x.shape
    n = b * s
    tables = _rope_tables(s)
    xf = x.reshape(n, d)
    h = _rmsnorm(xf, p["pre_norm_w"][0])
    for l in range(DEPTH):
        lam_init = 0.8 - 0.6 * math.exp(-0.3 * l)
        proj = _in_proj(h, p["w_in"], l)
        proj3 = proj.reshape(b, s, PROJ_COLS)
        dq, dk, dva, mq, mk, mva = _prep(proj3, tables, p["mla_q_norm_w"][l], p["mla_w_qb"][l],
                                         p["mla_kv_norm_w"][l], p["mla_w_kvb"][l])
        o_a = _na(proj3, p["na_bias"], l)
        o_b = _flash(dq, dk, dva, n_maps=2, lam_qk=p["diff_lambda_qk"][l],
                     subw=p["diff_subln_w"][l], lam_init=lam_init)
        o_c = _gla(proj3, p["gla_w_gate_up"][l], p["gla_b_gate"][l], p["gla_norm_w"][l])
        o_d = _flash(mq, mk, mva, n_maps=1)
        branches = [o.reshape(n, BRANCH_W) for o in (o_a, o_b, o_c, o_d)]
        merged = _merge(h, branches, proj, p["w_gate"], p["w_branch"], l)
        if l + 1 < DEPTH:
            xf, h = _out_proj(merged, p["w_out"], l, xf, p["post_norm_w"][l], p["pre_norm_w"][l + 1])
        else:
            (xf,) = _out_proj(merged, p["w_out"], l, xf, p["post_norm_w"][l])
    return xf.reshape(b, s, d)


def kernel(x_prompt, x_sample, pre_norm_w, w_in, na_rpb, diff_lambda_qk, diff_subln_w, gla_w_gate_up, gla_b_gate, gla_norm_w, mla_q_norm_w, mla_w_qb, mla_kv_norm_w, mla_w_kvb, w_gate, w_branch, w_out, post_norm_w):
    p = {
        "pre_norm_w": pre_norm_w,
        "w_in": _permute_w_in(w_in),
        "na_bias": _na_bias_tiles(na_rpb),
        "diff_lambda_qk": diff_lambda_qk,
        "diff_subln_w": diff_subln_w,
        "gla_w_gate_up": gla_w_gate_up,
        "gla_b_gate": gla_b_gate,
        "gla_norm_w": gla_norm_w,
        "mla_q_norm_w": mla_q_norm_w,
        "mla_w_qb": _pad_w_qb(mla_w_qb),
        "mla_kv_norm_w": mla_kv_norm_w,
        "mla_w_kvb": mla_w_kvb.astype(BF16),
        "w_gate": w_gate.astype(BF16),
        "w_branch": w_branch.astype(BF16),
        "w_out": w_out.astype(BF16),
        "post_norm_w": post_norm_w,
    }
    return (_trunk(x_prompt, p), _trunk(x_sample, p))
```

```python
import functools
import math

import jax
import jax.numpy as jnp
import numpy as np
from jax import lax
from jax.experimental import pallas as pl
from jax.experimental.pallas import tpu as pltpu

F32 = jnp.float32
BF16 = jnp.bfloat16

DEPTH = 4
GRID_W = 64
N_BRANCH = 4
BRANCH_W = 512
N_HEADS = 4
NA_HEAD_DIM = 128
WIN_R = 8
WIN_C = 16
DIFF_D = 64
GLA_DK = 64
GLA_DV = 128
GLA_RANK = 16
GLA_TAU = 16.0
GLA_CHUNK = 64
MLA_Q_LORA = 384
MLA_KV_LORA = 128
MLA_NOPE = 128
MLA_ROPE = 64
MLA_V = 128
ROPE_THETA = 10000.0
EPS = 1e-6
LOG2E = math.log2(math.e)

NA_COLS = 3 * N_HEADS * NA_HEAD_DIM
DIFF_COLS = 3 * N_HEADS * 2 * DIFF_D
GLA_QKV_COLS = 2 * N_HEADS * GLA_DK + N_HEADS * GLA_DV
GLA_COLS = GLA_QKV_COLS + 2 * GLA_RANK
MLA_COLS = MLA_Q_LORA + MLA_KV_LORA + MLA_ROPE
Z_COLS = N_BRANCH * BRANCH_W

LANES = 128
MIB = 1024 * 1024
V7X_VMEM_BYTES = 64 * MIB
VMEM_LIMIT = V7X_VMEM_BYTES - 8 * MIB

COL_NA = 0
COL_DIFF = COL_NA + NA_COLS
COL_GLA = COL_DIFF + DIFF_COLS
COL_MLA = COL_GLA + GLA_QKV_COLS
COL_Z = COL_MLA + MLA_Q_LORA + MLA_KV_LORA
COL_KRGL = COL_Z + Z_COLS
PROJ_TN = 1152
PROJ_COLS = 6 * PROJ_TN
GL_LANE0 = MLA_ROPE

NA_QROWS = 8
NA_KROWS = 16
NA_TQ = NA_QROWS * GRID_W
NA_TK = NA_KROWS * GRID_W
NA_KV_PARTS = 4
NEG_BIG = -1e30

VAUG_W = 2 * LANES
FLASH_TQ = 1024
FLASH_TK = 1024
FLASH_UNROLL = 2
MERGE_TN = 512
OUT_SUBTILES = 4
GLA_T = 256
GLA_ROWS_PER_STEP = 2


def _cparams(sem, vmem=None):
    return pltpu.CompilerParams(dimension_semantics=sem, vmem_limit_bytes=vmem)


def _nt_dot(a, b):
    return lax.dot_general(a, b, (((1,), (1,)), ((), ())), preferred_element_type=F32)


def _tn_dot(a, b):
    return lax.dot_general(a, b, (((0,), (0,)), ((), ())), preferred_element_type=F32)


def _rmsnorm_kernel(x_ref, w_ref, o_ref):
    x = x_ref[...]
    y = x * lax.rsqrt(jnp.mean(x * x, axis=-1, keepdims=True) + EPS)
    o_ref[...] = (y * w_ref[...]).astype(o_ref.dtype)


def _rmsnorm(x, w):
    n, d = x.shape
    tm = min(512, n)
    return pl.pallas_call(
        _rmsnorm_kernel,
        out_shape=jax.ShapeDtypeStruct((n, d), BF16),
        grid=(n // tm,),
        in_specs=[pl.BlockSpec((tm, d), lambda i: (i, 0)),
                  pl.BlockSpec((1, d), lambda i: (0, 0))],
        out_specs=pl.BlockSpec((tm, d), lambda i: (i, 0)),
        compiler_params=_cparams(("parallel",)),
        name="pre_norm",
    )(x, w.reshape(1, d))


def _matmul_kernel(a_ref, b_ref, o_ref):
    o_ref[...] = jnp.dot(a_ref[...], b_ref[...], preferred_element_type=F32).astype(o_ref.dtype)


def _in_proj(h, w, l):
    n, d = h.shape
    c = w.shape[2]
    tm = min(1024, n)
    tn = PROJ_TN
    return pl.pallas_call(
        _matmul_kernel,
        out_shape=jax.ShapeDtypeStruct((n, c), BF16),
        grid=(n // tm, c // tn),
        in_specs=[pl.BlockSpec((tm, d), lambda i, j: (i, 0)),
                  pl.BlockSpec((None, d, tn), lambda i, j: (l, 0, j))],
        out_specs=pl.BlockSpec((tm, tn), lambda i, j: (i, j)),
        compiler_params=_cparams(("parallel", "parallel"), VMEM_LIMIT),
        name="in_proj",
    )(h, w)


def _rope_lanes(x, c, s1, s2):
    return x * c + pltpu.roll(x, LANES - 32, 1) * s1 + pltpu.roll(x, 32, 1) * s2


def _prep_kernel(dq_ref, dk_ref, dv_ref, cqkv_ref, krgl_ref, cos_ref, s1_ref, s2_ref,
                 qnw_ref, wqb_ref, kvnw_ref, wkvb_ref,
                 dqo_ref, dko_ref, dvo_ref, mqo_ref, mko_ref, mvo_ref):
    c, s1, s2 = cos_ref[...], s1_ref[...], s2_ref[...]
    t = c.shape[0]
    lane = lax.broadcasted_iota(jnp.int32, (t, LANES), 1)
    ones_col = jnp.where(lane == 0, 1.0, 0.0).astype(BF16)
    diff_qscale = (DIFF_D ** -0.5) * LOG2E
    mla_qscale = ((MLA_NOPE + MLA_ROPE) ** -0.5) * LOG2E

    for hd in range(N_HEADS):
        sl = slice(hd * LANES, (hd + 1) * LANES)
        q = _rope_lanes(dq_ref[0, :, sl].astype(F32), c, s1, s2) * diff_qscale
        dqo_ref[0, :, sl] = q.astype(BF16)
        k = _rope_lanes(dk_ref[0, :, sl].astype(F32), c, s1, s2)
        dko_ref[0, :, sl] = k.astype(BF16)
        dvo_ref[0, :, hd * VAUG_W:hd * VAUG_W + LANES] = dv_ref[0, :, sl]
        dvo_ref[0, :, hd * VAUG_W + LANES:(hd + 1) * VAUG_W] = ones_col

    cq = cqkv_ref[0, :, :MLA_Q_LORA].astype(F32)
    cqn = cq * lax.rsqrt(jnp.mean(cq * cq, axis=-1, keepdims=True) + EPS) * qnw_ref[...]
    qm = jnp.dot(cqn.astype(BF16), wqb_ref[...], preferred_element_type=F32)
    ckv = cqkv_ref[0, :, MLA_Q_LORA:].astype(F32)
    ckvn = ckv * lax.rsqrt(jnp.mean(ckv * ckv, axis=-1, keepdims=True) + EPS) * kvnw_ref[...]
    kv = jnp.dot(ckvn.astype(BF16), wkvb_ref[...], preferred_element_type=F32)
    kr = jnp.where(lane < MLA_ROPE, krgl_ref[0].astype(F32), 0.0)
    kr = _rope_lanes(kr, c, s1, s2).astype(BF16)
    for hd in range(N_HEADS):
        lo, mid, hi = hd * VAUG_W, hd * VAUG_W + LANES, (hd + 1) * VAUG_W
        mqo_ref[0, :, lo:mid] = (qm[:, lo:mid] * mla_qscale).astype(BF16)
        qr = _rope_lanes(qm[:, mid:hi], c, s1, s2) * mla_qscale
        mqo_ref[0, :, mid:hi] = qr.astype(BF16)
        mko_ref[0, :, lo:mid] = kv[:, lo:mid].astype(BF16)
        mko_ref[0, :, mid:hi] = kr
        mvo_ref[0, :, lo:mid] = kv[:, mid:hi].astype(BF16)
        mvo_ref[0, :, mid:hi] = ones_col


def _prep(proj3, tables, qnw, wqb, kvnw, wkvb):
    b, s, _ = proj3.shape
    tm = min(512, s)
    cos, s1, s2 = tables
    hw = N_HEADS * LANES
    aw = N_HEADS * VAUG_W
    tok = lambda cb: (lambda bi, i: (bi, i, cb))
    tab = pl.BlockSpec((tm, LANES), lambda bi, i: (i, 0))
    const = lambda shape: pl.BlockSpec(shape, lambda bi, i: (0, 0))
    out_sd = lambda w: jax.ShapeDtypeStruct((b, s, w), BF16)
    out_bs = lambda w: pl.BlockSpec((1, tm, w), lambda bi, i: (bi, i, 0))
    return pl.pallas_call(
        _prep_kernel,
        out_shape=(out_sd(hw), out_sd(hw), out_sd(aw), out_sd(aw), out_sd(aw), out_sd(aw)),
        grid=(b, s // tm),
        in_specs=[pl.BlockSpec((1, tm, hw), tok(COL_DIFF // hw)),
                  pl.BlockSpec((1, tm, hw), tok(COL_DIFF // hw + 1)),
                  pl.BlockSpec((1, tm, hw), tok(COL_DIFF // hw + 2)),
                  pl.BlockSpec((1, tm, hw), tok(COL_MLA // hw)),
                  pl.BlockSpec((1, tm, LANES), tok(COL_KRGL // LANES)),
                  tab, tab, tab,
                  const((1, MLA_Q_LORA)), const(wqb.shape),
                  const((1, MLA_KV_LORA)), const(wkvb.shape)],
        out_specs=(out_bs(hw), out_bs(hw), out_bs(aw), out_bs(aw), out_bs(aw), out_bs(aw)),
        compiler_params=_cparams(("parallel", "parallel"), VMEM_LIMIT),
        name="branch_prep",
    )(proj3, proj3, proj3, proj3, proj3, cos, s1, s2,
      qnw.reshape(1, -1), wqb, kvnw.reshape(1, -1), wkvb)


def _flash_kernel(*refs, n_maps, tk, lam_init):
    if n_maps == 2:
        q_ref, kt_ref, v_ref, lam_ref, subw_ref, o_ref, m_scr, acc_scr, s_scr = refs
    else:
        q_ref, kt_ref, v_ref, o_ref, m_scr, acc_scr, s_scr = refs
    nk = kt_ref.shape[2] // tk
    assert nk >= 2 and nk % 2 == 0
    q = q_ref[0]
    if n_maps == 2:
        lane = lax.broadcasted_iota(jnp.int32, q.shape, 1)
        zero = jnp.zeros_like(q)
        qs = (jnp.where(lane < DIFF_D, q, zero), jnp.where(lane >= DIFF_D, q, zero))
    else:
        qs = (q,)

    m_scr[...] = jnp.full(m_scr.shape, -jnp.inf, F32)
    acc_scr[...] = jnp.zeros(acc_scr.shape, F32)

    def scores(j, slot):
        kt = kt_ref[0, :, pl.ds(pl.multiple_of(j * tk, tk), tk)]
        for mi in range(n_maps):
            s_scr[slot, mi] = jnp.dot(qs[mi], kt, preferred_element_type=F32)

    def accumulate(j, slot):
        v = v_ref[0, pl.ds(pl.multiple_of(j * tk, tk), tk), :]
        for mi in range(n_maps):
            s = s_scr[slot, mi]
            m_prev = m_scr[mi]
            m_new = jnp.maximum(m_prev, jnp.max(s, axis=1, keepdims=True))
            alpha = jnp.exp2(m_prev - m_new)
            p = jnp.exp2(s - jnp.tile(m_new, (1, tk // LANES)))
            pv = jnp.dot(p.astype(BF16), v, preferred_element_type=F32)
            acc_scr[mi] = acc_scr[mi] * jnp.tile(alpha, (1, VAUG_W // LANES)) + pv
            m_scr[mi] = m_new

    unroll = min(FLASH_UNROLL, nk)
    assert nk % unroll == 0 and unroll % 2 == 0
    scores(0, 0)

    def body(i, carry):
        j0 = unroll * i
        for t in range(unroll):
            scores(j0 + t + 1, (t + 1) % 2)
            accumulate(j0 + t, t % 2)
        return carry

    lax.fori_loop(0, nk // unroll - 1, body, 0)
    j0 = nk - unroll
    for t in range(unroll):
        if t + 1 < unroll:
            scores(j0 + t + 1, (t + 1) % 2)
        accumulate(j0 + t, t % 2)

    outs = []
    for mi in range(n_maps):
        acc = acc_scr[mi]
        outs.append(acc[:, :LANES] / acc[:, LANES:LANES + 1])
    if n_maps == 2:
        lq = lam_ref[...]
        lam = (jnp.exp(jnp.sum(lq[0:1] * lq[1:2], axis=1, keepdims=True))
               - jnp.exp(jnp.sum(lq[2:3] * lq[3:4], axis=1, keepdims=True)) + lam_init)
        o = outs[0] - lam * outs[1]
        y = o * lax.rsqrt(jnp.mean(o * o, axis=-1, keepdims=True) + EPS) * subw_ref[...]
        o = y * (1.0 - lam_init)
    else:
        o = outs[0]
    o_ref[0] = o.astype(o_ref.dtype)


def _flash(q, k, vaug, *, n_maps, lam_qk=None, subw=None, lam_init=0.0):
    b, s, qw_all = q.shape
    qw = qw_all // N_HEADS
    kw = k.shape[2] // N_HEADS
    kt = jnp.swapaxes(k, 1, 2)
    tq = min(FLASH_TQ, s)
    tk = min(FLASH_TK, s // 2)
    in_specs = [pl.BlockSpec((1, tq, qw), lambda bi, hd, i: (bi, i, hd)),
                pl.BlockSpec((1, kw, s), lambda bi, hd, i: (bi, hd, 0)),
                pl.BlockSpec((1, s, VAUG_W), lambda bi, hd, i: (bi, 0, hd))]
    args = [q, kt, vaug]
    if n_maps == 2:
        in_specs += [pl.BlockSpec(lam_qk.shape, lambda bi, hd, i: (0, 0)),
                     pl.BlockSpec((1, LANES), lambda bi, hd, i: (0, 0))]
        args += [lam_qk, subw.reshape(1, LANES)]
    return pl.pallas_call(
        functools.partial(_flash_kernel, n_maps=n_maps, tk=tk, lam_init=lam_init),
        out_shape=jax.ShapeDtypeStruct((b, s, N_HEADS * LANES), BF16),
        grid=(b, N_HEADS, s // tq),
        in_specs=in_specs,
        out_specs=pl.BlockSpec((1, tq, LANES), lambda bi, hd, i: (bi, i, hd)),
        scratch_shapes=[pltpu.VMEM((n_maps, tq, LANES), F32),
                        pltpu.VMEM((n_maps, tq, VAUG_W), F32),
                        pltpu.VMEM((2, n_maps, tq, tk), F32)],
        compiler_params=_cparams(("parallel", "parallel", "parallel"), VMEM_LIMIT),
        name="diff_attention" if n_maps == 2 else "latent_attention",
    )(*args)


def _na_kernel(q_ref, *refs):
    k_refs, v_refs = refs[:NA_KV_PARTS], refs[NA_KV_PARTS:2 * NA_KV_PARTS]
    b_ref, o_ref = refs[2 * NA_KV_PARTS:]
    for hd in range(N_HEADS):
        sl = slice(hd * LANES, (hd + 1) * LANES)
        k = jnp.concatenate([r[0, :, sl] for r in k_refs], axis=0)
        v = jnp.concatenate([r[0, :, sl] for r in v_refs], axis=0)
        s = _nt_dot(q_ref[0, :, sl], k) * (NA_HEAD_DIM ** -0.5) + b_ref[0, hd].astype(F32)
        m = jnp.max(s, axis=1, keepdims=True)
        p = jnp.exp(s - m)
        l = jnp.sum(p, axis=1, keepdims=True)
        o = jnp.dot(p.astype(BF16), v, preferred_element_type=F32) / l
        o_ref[0, :, sl] = o.astype(o_ref.dtype)


def _na_tile_kernel(toep_ref, o_ref):
    off = pl.program_id(1) * (WIN_R // 2)
    for a in range(NA_QROWS):
        lo = jnp.clip(a + off - WIN_R // 2, 0, NA_KROWS - WIN_R)
        pieces = []
        for w in range(NA_KROWS):
            d = w - a - off + (WIN_R - 1)
            piece = toep_ref[jnp.clip(d, 0, 2 * WIN_R - 2)]
            pieces.append(jnp.where((w >= lo) & (w < lo + WIN_R), piece, NEG_BIG))
        o_ref[a * GRID_W:(a + 1) * GRID_W, :] = jnp.concatenate(pieces, axis=1).astype(o_ref.dtype)


def _na_bias_tiles(rpb):
    c = np.arange(GRID_W)
    cs = np.clip(c - WIN_C // 2, 0, GRID_W - WIN_C)
    col_ok = (c[None, :] >= cs[:, None]) & (c[None, :] < cs[:, None] + WIN_C)
    dc = c[None, :] - c[:, None] + (WIN_C - 1)
    sel_c = ((dc[..., None] == np.arange(2 * WIN_C - 1)) & col_ok[..., None]).astype(np.float32)
    toep = jnp.einsum("qkc,lhdc->lhdqk", sel_c, rpb.astype(F32), precision=lax.Precision.HIGHEST)
    toep = jnp.where(col_ok, toep, NEG_BIG)
    depth = rpb.shape[0]
    return pl.pallas_call(
        _na_tile_kernel,
        out_shape=jax.ShapeDtypeStruct((depth, 3, N_HEADS, NA_TQ, NA_TK), BF16),
        grid=(depth, 3, N_HEADS),
        in_specs=[pl.BlockSpec((None, None, 2 * WIN_R - 1, GRID_W, GRID_W), lambda l, t, h: (l, h, 0, 0, 0))],
        out_specs=pl.BlockSpec((None, None, None, NA_TQ, NA_TK), lambda l, t, h: (l, t, h, 0, 0)),
        compiler_params=_cparams(("parallel", "parallel", "parallel")),
        name="na_bias_tiles",
    )(toep)


def _na(proj3, bias_tiles, l):
    b, s, _ = proj3.shape
    rows = s // GRID_W
    assert rows % NA_QROWS == 0 and rows >= NA_KROWS
    nq = rows // NA_QROWS
    hw = N_HEADS * LANES
    part_rows = NA_KROWS // NA_KV_PARTS
    part = part_rows * GRID_W
    qb = COL_NA // hw

    def kv_spec(cb, t):
        def idx(i, bi):
            ks_row = jnp.clip(i * NA_QROWS - WIN_R // 2, 0, rows - NA_KROWS)
            return (bi, ks_row // part_rows + t, cb)
        return pl.BlockSpec((1, part, hw), idx)

    def bias_idx(i, bi):
        kind = jnp.where(i == 0, 0, jnp.where(i == nq - 1, 2, 1))
        return (l, kind, 0, 0, 0)

    return pl.pallas_call(
        _na_kernel,
        out_shape=jax.ShapeDtypeStruct((b, s, hw), BF16),
        grid=(nq, b),
        in_specs=([pl.BlockSpec((1, NA_TQ, hw), lambda i, bi: (bi, i, qb))]
                  + [kv_spec(qb + 1, t) for t in range(NA_KV_PARTS)]
                  + [kv_spec(qb + 2, t) for t in range(NA_KV_PARTS)]
                  + [pl.BlockSpec((None, 1, N_HEADS, NA_TQ, NA_TK), bias_idx)]),
        out_specs=pl.BlockSpec((1, NA_TQ, hw), lambda i, bi: (bi, i, 0)),
        compiler_params=_cparams(("parallel", "parallel"), VMEM_LIMIT),
        name="neighborhood_attention",
    )(proj3, *([proj3] * (2 * NA_KV_PARTS)), bias_tiles)


def _gla_kernel(*refs, reverse):
    if reverse:
        q_ref, k_ref, v_ref, gl_ref, wg_ref, bg_ref, ofwd_ref, nw_ref, o_ref, st_scr = refs
    else:
        q_ref, k_ref, v_ref, gl_ref, wg_ref, bg_ref, o_ref, st_scr = refs
    rows_per_step, t = q_ref.shape[0], q_ref.shape[1]
    gw = N_HEADS * GLA_DK
    c_len = GLA_CHUNK

    @pl.when(pl.program_id(1) == 0)
    def _():
        st_scr[...] = jnp.zeros(st_scr.shape, F32)

    pos = lax.broadcasted_iota(jnp.int32, (t, gw), 0) & (c_len - 1)
    ref_i = c_len // 2 if reverse else c_len // 2 - 1
    last_i = 0 if reverse else c_len - 1
    n_chunks = t // c_len
    order = range(n_chunks - 1, -1, -1) if reverse else range(n_chunks)
    ri = lax.broadcasted_iota(jnp.int32, (t, t), 0)
    ci = lax.broadcasted_iota(jnp.int32, (t, t), 1)
    same_chunk = (ri // c_len) == (ci // c_len)
    intra = same_chunk & ((ci >= ri) if reverse else (ri >= ci))
    lane = lax.broadcasted_iota(jnp.int32, (t, LANES), 1)
    row_chunk = lax.broadcasted_iota(jnp.int32, (t, LANES), 0) // c_len

    def spread(x):
        return jnp.concatenate([jnp.where(row_chunk == n, x, 0.0) for n in range(n_chunks)], axis=1)

    def per_chunk_rows(x, i):
        return jnp.concatenate([jnp.broadcast_to(x[n * c_len + i:n * c_len + i + 1], (c_len, x.shape[1]))
                                for n in range(n_chunks)], axis=0)

    for bi in range(rows_per_step):
        gpre = jnp.dot(gl_ref[bi], wg_ref[...], preferred_element_type=F32) + bg_ref[...]
        g = (jnp.minimum(gpre, 0.0) - jnp.log1p(jnp.exp(-jnp.abs(gpre)))) * (1.0 / GLA_TAU)

        cum = g
        for sh in (1, 2, 4, 8, 16, 32):
            if reverse:
                cum = cum + jnp.where(pos < c_len - sh, pltpu.roll(cum, t - sh, 0), 0.0)
            else:
                cum = cum + jnp.where(pos >= sh, pltpu.roll(cum, sh, 0), 0.0)

        ref = per_chunk_rows(cum, ref_i)
        last = per_chunk_rows(cum, last_i)
        qf = q_ref[bi].astype(F32) * (GLA_DK ** -0.5)
        kf = k_ref[bi].astype(F32)
        qd = qf * jnp.exp(cum - ref)
        kd = (kf * jnp.exp(ref - cum)).astype(BF16)
        qe = qf * jnp.exp(cum)
        kl = kf * jnp.exp(last - cum)
        dec = jnp.exp(last)

        for pr in range(N_HEADS // 2):
            lsl = slice(pr * LANES, (pr + 1) * LANES)
            owns = (lane < GLA_DK, lane >= GLA_DK)
            qd_p, qe_p = qd[:, lsl], qe[:, lsl]
            lhs = jnp.concatenate([jnp.where(own, qd_p, 0.0) for own in owns], axis=0).astype(BF16)
            a_pair = _nt_dot(lhs, kd[:, lsl])
            kl_spread = spread(kl[:, lsl]).astype(BF16)
            for hh in range(2):
                hd = 2 * pr + hh
                vsl = slice(hd * GLA_DV, (hd + 1) * GLA_DV)
                v_h = v_ref[bi, :, vsl]
                a = jnp.where(intra, a_pair[hh * t:(hh + 1) * t], 0.0)
                o_h = jnp.dot(a.astype(BF16), v_h, preferred_element_type=F32)
                ds_all = _tn_dot(v_h, kl_spread)
                st = st_scr[bi, hd]
                before = [None] * n_chunks
                for n in order:
                    before[n] = st
                    dec_n = dec[n * c_len:n * c_len + 1, lsl]
                    st = st * dec_n + ds_all[:, n * LANES:(n + 1) * LANES]
                st_scr[bi, hd] = st
                st_cat = jnp.concatenate(before, axis=1).astype(BF16)
                qe_spread = spread(jnp.where(owns[hh], qe_p, 0.0)).astype(BF16)
                o_h = o_h + _nt_dot(qe_spread, st_cat)
                if reverse:
                    tot = o_h + ofwd_ref[bi, :, vsl]
                    y = tot * lax.rsqrt(jnp.mean(tot * tot, axis=-1, keepdims=True) + EPS) * nw_ref[...]
                    o_ref[bi, :, vsl] = y.astype(o_ref.dtype)
                else:
                    o_ref[bi, :, vsl] = o_h


def _gla_pass(proj3, wg, bg, reverse, o_fwd=None, norm_w=None):
    b, s, _ = proj3.shape
    t = min(GLA_T, s)
    nb = s // t
    gw = N_HEADS * GLA_DK
    vw = N_HEADS * GLA_DV
    blk = (lambda i: nb - 1 - i) if reverse else (lambda i: i)
    tok = lambda cb: (lambda bi, i: (bi, blk(i), cb))
    const = lambda shape: pl.BlockSpec(shape, lambda bi, i: (0, 0))
    rb = GLA_ROWS_PER_STEP
    assert b % rb == 0
    in_specs = [pl.BlockSpec((rb, t, gw), tok(COL_GLA // gw)),
                pl.BlockSpec((rb, t, gw), tok(COL_GLA // gw + 1)),
                pl.BlockSpec((rb, t, vw), tok((COL_GLA + 2 * gw) // vw)),
                pl.BlockSpec((rb, t, LANES), tok(COL_KRGL // LANES)),
                const((LANES, gw)), const((1, gw))]
    args = [proj3, proj3, proj3, proj3, wg, bg]
    if reverse:
        in_specs += [pl.BlockSpec((rb, t, vw), tok(0)), const((1, GLA_DV))]
        args += [o_fwd, norm_w.reshape(1, GLA_DV)]
    return pl.pallas_call(
        functools.partial(_gla_kernel, reverse=reverse),
        out_shape=jax.ShapeDtypeStruct((b, s, vw), BF16 if reverse else F32),
        grid=(b // rb, nb),
        in_specs=in_specs,
        out_specs=pl.BlockSpec((rb, t, vw), tok(0)),
        scratch_shapes=[pltpu.VMEM((rb, N_HEADS, GLA_DV, LANES), F32)],
        compiler_params=_cparams(("parallel", "arbitrary")),
        name="gla_backward" if reverse else "gla_forward",
    )(*args)


def _gla(proj3, w_gate_up, b_gate, norm_w):
    gw = N_HEADS * GLA_DK
    wgs = []
    for d in range(2):
        lo = GL_LANE0 + d * GLA_RANK
        wgs.append(jnp.zeros((LANES, gw), F32).at[lo:lo + GLA_RANK].set(w_gate_up[d]).astype(BF16))
    o_fwd = _gla_pass(proj3, wgs[0], b_gate[0].reshape(1, gw), False)
    return _gla_pass(proj3, wgs[1], b_gate[1].reshape(1, gw), True, o_fwd, norm_w)


def _merge_kernel(h_ref, oa_ref, ob_ref, oc_ref, od_ref, za_ref, zb_ref, zc_ref, zd_ref,
                  wg_ref, wb_ref, o_ref):
    h = h_ref[...]
    acc = None
    for i, (o_r, z_r) in enumerate(((oa_ref, za_ref), (ob_ref, zb_ref), (oc_ref, zc_ref), (od_ref, zd_ref))):
        z = z_r[...].astype(F32)
        u = (o_r[...].astype(F32) * (z * jax.nn.sigmoid(z))).astype(BF16)
        t = jnp.dot(u, wb_ref[i], preferred_element_type=F32)
        gate = jax.nn.sigmoid(jnp.dot(h, wg_ref[i], preferred_element_type=F32))
        acc = gate * t if acc is None else acc + gate * t
    o_ref[...] = acc.astype(o_ref.dtype)


def _merge(h, branches, proj, wg, wb, l):
    n, d = h.shape
    tm = min(512, n)
    tn = min(MERGE_TN, d)
    zb0 = COL_Z // BRANCH_W
    tok = pl.BlockSpec((tm, BRANCH_W), lambda j, i: (i, 0))
    zspec = lambda k: pl.BlockSpec((tm, BRANCH_W), lambda j, i: (i, zb0 + k))
    return pl.pallas_call(
        _merge_kernel,
        out_shape=jax.ShapeDtypeStruct((n, d), BF16),
        grid=(d // tn, n // tm),
        in_specs=[pl.BlockSpec((tm, d), lambda j, i: (i, 0)), tok, tok, tok, tok,
                  zspec(0), zspec(1), zspec(2), zspec(3),
                  pl.BlockSpec((None, N_BRANCH, d, tn), lambda j, i: (l, 0, 0, j)),
                  pl.BlockSpec((None, N_BRANCH, BRANCH_W, tn), lambda j, i: (l, 0, 0, j))],
        out_specs=pl.BlockSpec((tm, tn), lambda j, i: (i, j)),
        compiler_params=_cparams(("parallel", "parallel"), VMEM_LIMIT),
        name="branch_merge",
    )(h, *branches, proj, proj, proj, proj, wg, wb)


def _out_kernel(m_ref, w_ref, x_ref, pw_ref, *rest):
    if len(rest) == 1:
        (xo_ref,), nw_ref, ho_ref = rest, None, None
    else:
        nw_ref, xo_ref, ho_ref = rest
    sub = m_ref.shape[0] // OUT_SUBTILES
    for r in range(OUT_SUBTILES):
        rows = slice(r * sub, (r + 1) * sub)
        y = jnp.dot(m_ref[rows, :], w_ref[...], preferred_element_type=F32)
        y = y * lax.rsqrt(jnp.mean(y * y, axis=-1, keepdims=True) + EPS) * pw_ref[...]
        xn = x_ref[rows, :] + y
        if ho_ref is not None:
            hn = xn * lax.rsqrt(jnp.mean(xn * xn, axis=-1, keepdims=True) + EPS) * nw_ref[...]
            ho_ref[rows, :] = hn.astype(ho_ref.dtype)
        xo_ref[rows, :] = xn


def _out_proj(merged, w_out, l, x, post_w, next_pre_w=None):
    n, d = x.shape
    tm = min(512, n)
    tok = lambda: pl.BlockSpec((tm, d), lambda i: (i, 0))
    vec = lambda: pl.BlockSpec((1, d), lambda i: (0, 0))
    in_specs = [tok(), pl.BlockSpec((None, d, d), lambda i: (l, 0, 0)), tok(), vec()]
    args = [merged, w_out, x, post_w.reshape(1, d)]
    out_shape = [jax.ShapeDtypeStruct((n, d), F32)]
    out_specs = [tok()]
    if next_pre_w is not None:
        in_specs.append(vec())
        args.append(next_pre_w.reshape(1, d))
        out_shape.append(jax.ShapeDtypeStruct((n, d), BF16))
        out_specs.append(tok())
    return pl.pallas_call(
        _out_kernel,
        out_shape=tuple(out_shape),
        grid=(n // tm,),
        in_specs=in_specs,
        out_specs=tuple(out_specs),
        compiler_params=_cparams(("parallel",), VMEM_LIMIT),
        name="out_proj",
    )(*args)


def _permute_w_in(w_in):
    depth, d, _ = w_in.shape
    w_in = w_in.astype(BF16)
    na = w_in[:, :, :NA_COLS]
    diff = w_in[:, :, NA_COLS:NA_COLS + DIFF_COLS]
    g0 = NA_COLS + DIFF_COLS
    gla_qkv = w_in[:, :, g0:g0 + GLA_QKV_COLS]
    gl = w_in[:, :, g0 + GLA_QKV_COLS:g0 + GLA_COLS]
    m0 = g0 + GLA_COLS
    cq_ckv = w_in[:, :, m0:m0 + MLA_Q_LORA + MLA_KV_LORA]
    kr = w_in[:, :, m0 + MLA_Q_LORA + MLA_KV_LORA:m0 + MLA_COLS]
    z = w_in[:, :, m0 + MLA_COLS:]
    used = COL_KRGL + MLA_ROPE + 2 * GLA_RANK
    pad = jnp.zeros((depth, d, PROJ_COLS - used), w_in.dtype)
    return jnp.concatenate([na, diff, gla_qkv, cq_ckv, z, kr, gl, pad], axis=-1).astype(BF16)


def _pad_w_qb(w_qb):
    depth, r, _ = w_qb.shape
    w = w_qb.reshape(depth, r, N_HEADS, MLA_NOPE + MLA_ROPE)
    w = jnp.pad(w, ((0, 0), (0, 0), (0, 0), (0, VAUG_W - MLA_NOPE - MLA_ROPE)))
    return w.reshape(depth, r, N_HEADS * VAUG_W).astype(BF16)


def _rope_tables(s):
    half = DIFF_D // 2
    inv = ROPE_THETA ** (-jnp.arange(0, DIFF_D, 2, dtype=F32) / DIFF_D)
    ang = jnp.arange(s, dtype=F32)[:, None] * inv[None, :]
    cos, sin = jnp.cos(ang), jnp.sin(ang)
    zero = jnp.zeros_like(sin)
    reps = LANES // DIFF_D
    c = jnp.tile(jnp.concatenate([cos, cos], axis=1), (1, reps))
    s1 = jnp.tile(jnp.concatenate([-sin, zero], axis=1), (1, reps))
    s2 = jnp.tile(jnp.concatenate([zero, sin], axis=1), (1, reps))
    assert half * 2 * reps == LANES
    return c, s1, s2


def _trunk(x, p):
    b, s, d = x.shape
    n = b * s
    tables = _rope_tables(s)
    xf = x.reshape(n, d)
    h = _rmsnorm(xf, p["pre_norm_w"][0])
    for l in range(DEPTH):
        lam_init = 0.8 - 0.6 * math.exp(-0.3 * l)
        proj = _in_proj(h, p["w_in"], l)
        proj3 = proj.reshape(b, s, PROJ_COLS)
        dq, dk, dva, mq, mk, mva = _prep(proj3, tables, p["mla_q_norm_w"][l], p["mla_w_qb"][l],
                                         p["mla_kv_norm_w"][l], p["mla_w_kvb"][l])
        o_a = _na(proj3, p["na_bias"], l)
        o_b = _flash(dq, dk, dva, n_maps=2, lam_qk=p["diff_lambda_qk"][l],
                     subw=p["diff_subln_w"][l], lam_init=lam_init)
        o_c = _gla(proj3, p["gla_w_gate_up"][l], p["gla_b_gate"][l], p["gla_norm_w"][l])
        o_d = _flash(mq, mk, mva, n_maps=1)
        branches = [o.reshape(n, BRANCH_W) for o in (o_a, o_b, o_c, o_d)]
        merged = _merge(h, branches, proj, p["w_gate"], p["w_branch"], l)
        if l + 1 < DEPTH:
            xf, h = _out_proj(merged, p["w_out"], l, xf, p["post_norm_w"][l], p["pre_norm_w"][l + 1])
        else:
            (xf,) = _out_proj(merged, p["w_out"], l, xf, p["post_norm_w"][l])
    return xf.reshape(b, s, d)


def kernel(x_prompt, x_sample, pre_norm_w, w_in, na_rpb, diff_lambda_qk, diff_subln_w, gla_w_gate_up, gla_b_gate, gla_norm_w, mla_q_norm_w, mla_w_qb, mla_kv_norm_w, mla_w_kvb, w_gate, w_branch, w_out, post_norm_w):
    p = {
        "pre_norm_w": pre_norm_w,
        "w_in": _permute_w_in(w_in),
        "na_bias": _na_bias_tiles(na_rpb),
        "diff_lambda_qk": diff_lambda_qk,
        "diff_subln_w": diff_subln_w,
        "gla_w_gate_up": gla_w_gate_up,
        "gla_b_gate": gla_b_gate,
        "gla_norm_w": gla_norm_w,
        "mla_q_norm_w": mla_q_norm_w,
        "mla_w_qb": _pad_w_qb(mla_w_qb),
        "mla_kv_norm_w": mla_kv_norm_w,
        "mla_w_kvb": mla_w_kvb.astype(BF16),
        "w_gate": w_gate.astype(BF16),
        "w_branch": w_branch.astype(BF16),
        "w_out": w_out.astype(BF16),
        "post_norm_w": post_norm_w,
    }
    return (_trunk(x_prompt, p), _trunk(x_sample, p))
```
